```python
import jax, jax.numpy as jnp
from jax import lax
import numpy as np


D_MODEL = 2048
BATCH = 2
SEQ = 16384
DEPTH = 1
DEC_BATCH = 2
DEC_SEQ = 8192
PAST_LEN = 128

ATTN_HEADS = 8
ATTN_HEAD_DIM = D_MODEL // 16
ATTN_WIDTH = ATTN_HEADS * ATTN_HEAD_DIM
DN_HEADS = 8
DN_HEAD_DIM = D_MODEL // 16
DN_WIDTH = DN_HEADS * DN_HEAD_DIM
MIX_WIDTH = ATTN_WIDTH + DN_WIDTH
DILATION_PATTERNS = ((128, 1), (512, 4), (2048, 16))
N_BUCKETS = 32
MAX_DISTANCE = 1024
CONV_K = 5
DN_CHUNK = 64
D_FF = 256 * ((8 * D_MODEL // 3 + 255) // 256)
FFN_RESIDUAL = 0.5
N_MOD = 9
IN_COLS = 3 * ATTN_WIDTH + 4 * DN_WIDTH + 4 * DN_HEADS
EPS = 1e-6
NEG_INF = -1e30

kernel_name = 'hybrid_dilated_attn_gated_deltanet_encoder'


def rms_norm(x, w):
    xf = x.astype(jnp.float32)
    y = xf * lax.rsqrt(jnp.mean(xf * xf, axis=-1, keepdims=True) + EPS)
    return (y * w.astype(jnp.float32)).astype(x.dtype)


def modulate(h, shift, scale):
    return h * (1 + scale) + shift


def swiglu(h, w_in, w_out):
    g, u = jnp.split(h @ w_in, 2, axis=-1)
    return (jax.nn.silu(g) * u) @ w_out


def l2_normalize(t):
    return t * lax.rsqrt(jnp.sum(t * t, axis=-1, keepdims=True) + EPS)


def t5_bucket(rel):
    half = N_BUCKETS // 2
    max_exact = half // 2
    n = np.abs(rel)
    large = max_exact + (np.log(np.maximum(n, 1) / max_exact) / np.log(MAX_DISTANCE / max_exact) * (half - max_exact)).astype(np.int32)
    large = np.minimum(large, half - 1)
    return (np.where(rel > 0, half, 0) + np.where(n < max_exact, n, large)).astype(np.int32)


def dilated_window_attention(q, k, v, rel_bias, window, dilation):
    b, s, h, dh = q.shape
    side = window // (2 * dilation)
    sub_len = s // dilation
    n_blk = -(-sub_len // side)
    pad_len = n_blk * side - sub_len
    bd = b * dilation

    def to_residue(t):
        return t.reshape(b, sub_len, dilation, h, dh).transpose(0, 2, 1, 3, 4).reshape(bd, sub_len, h, dh)

    def band(t):
        t = jnp.pad(t, ((0, 0), (side, pad_len + side), (0, 0), (0, 0))).reshape(bd, n_blk + 2, side, h, dh)
        return jnp.concatenate([t[:, :-2], t[:, 1:-1], t[:, 2:]], axis=2)

    qb = jnp.pad(to_residue(q), ((0, 0), (0, pad_len), (0, 0), (0, 0))).reshape(bd, n_blk, side, h, dh)
    kb = band(to_residue(k))
    vb = band(to_residue(v))

    rel = np.arange(3 * side)[None, :] - side - np.arange(side)[:, None]
    key_pos = np.arange(n_blk)[:, None] * side + np.arange(3 * side)[None, :] - side
    allowed = (np.abs(rel) <= side)[None] & ((key_pos >= 0) & (key_pos < sub_len))[:, None, :]
    bias = jnp.transpose(rel_bias[t5_bucket(rel * dilation)], (2, 0, 1)).astype(jnp.float32)

    logits = jnp.einsum('bnqhd,bnkhd->bnhqk', qb, kb, preferred_element_type=jnp.float32) * (dh ** -0.5) + bias
    logits = jnp.where(jnp.asarray(allowed)[None, :, None], logits, NEG_INF)
    m = jnp.max(logits, axis=-1, keepdims=True)
    p = jnp.exp(logits - m)
    den = jnp.sum(p, axis=-1, keepdims=True)
    o = jnp.einsum('bnhqk,bnkhd->bnqhd', p / den, vb.astype(jnp.float32))
    lse = jnp.swapaxes((m + jnp.log(den))[..., 0], 2, 3)
    o = o.reshape(bd, n_blk * side, h, dh)[:, :sub_len].reshape(b, dilation, sub_len, h, dh).transpose(0, 2, 1, 3, 4).reshape(b, s, h, dh)
    lse = lse.reshape(bd, n_blk * side, h)[:, :sub_len].reshape(b, dilation, sub_len, h).transpose(0, 2, 1, 3).reshape(b, s, h)
    return o, lse


def gated_delta_rule(q, k, v, beta, g):
    b, s, h, dk = q.shape
    dv = v.shape[-1]
    c = DN_CHUNK
    n = s // c

    def chunks(t):
        return jnp.moveaxis(t.reshape(b, n, c, h, *t.shape[3:]), 3, 1)

    q, k, v, beta, g = map(chunks, (q, k, v, beta, g))
    q = q * (dk ** -0.5)
    gc = jnp.cumsum(g, axis=-1)
    tri_incl = jnp.tril(jnp.ones((c, c), dtype=bool))
    tri_strict = jnp.tril(jnp.ones((c, c), dtype=bool), -1)
    decay = jnp.exp(jnp.where(tri_incl, gc[..., :, None] - gc[..., None, :], -jnp.inf))
    kk = jnp.einsum('bhnik,bhnjk->bhnij', k, k)
    lower = jnp.where(tri_strict, beta[..., :, None] * kk * decay, 0.0)
    a = lower + jnp.eye(c, dtype=lower.dtype)
    rhs = jnp.concatenate([v * beta[..., None], k * (beta * jnp.exp(gc))[..., None]], axis=-1)
    sol = lax.linalg.triangular_solve(a, rhs, left_side=True, lower=True)
    u, w = sol[..., :dv], sol[..., dv:]
    aqk = jnp.where(tri_incl, jnp.einsum('bhnik,bhnjk->bhnij', q, k) * decay, 0.0)
    q_dec = q * jnp.exp(gc)[..., None]
    k_dec = k * jnp.exp(gc[..., -1:] - gc)[..., None]
    g_last = jnp.exp(gc[..., -1])

    def step(state, inp):
        u_c, w_c, aqk_c, qd_c, kd_c, gl_c = inp
        v_new = u_c - jnp.einsum('bhck,bhkv->bhcv', w_c, state)
        o_c = jnp.einsum('bhck,bhkv->bhcv', qd_c, state) + jnp.einsum('bhij,bhjv->bhiv', aqk_c, v_new)
        state = state * gl_c[..., None, None] + jnp.einsum('bhck,bhcv->bhkv', kd_c, v_new)
        return state, o_c

    xs = tuple(jnp.moveaxis(t, 2, 0) for t in (u, w, aqk, q_dec, k_dec, g_last))
    _, o = lax.scan(step, jnp.zeros((b, h, dk, dv), jnp.float32), xs)
    return jnp.transpose(o, (1, 0, 3, 2, 4)).reshape(b, s, h, dv)


def centred_depthwise_conv(x, w):
    kw, ch = w.shape
    return lax.conv_general_dilated(x, w.astype(x.dtype).reshape(kw, 1, ch), window_strides=(1,), padding=[(kw // 2, kw // 2)], dimension_numbers=('NWC', 'WIO', 'NWC'), feature_group_count=ch)


def deltanet_mixer(dq, dk, dv, z, beta_raw, alpha_raw, conv_w, a_log, dt_bias, norm_dn_out):
    b, s, _ = dq.shape
    qkv = jnp.concatenate([dq, dk, dv], axis=-1).astype(jnp.float32)
    qkv = jax.nn.silu(centred_depthwise_conv(qkv, conv_w))
    q, k, v = [t.reshape(b, s, DN_HEADS, DN_HEAD_DIM) for t in jnp.split(qkv, 3, axis=-1)]
    q, k = l2_normalize(q), l2_normalize(k)
    beta = jax.nn.sigmoid(beta_raw.astype(jnp.float32)).reshape(b, s, 2, DN_HEADS)
    g = -jnp.exp(a_log.astype(jnp.float32)) * jax.nn.softplus(alpha_raw.astype(jnp.float32).reshape(b, s, 2, DN_HEADS) + dt_bias.astype(jnp.float32))
    o_fwd = gated_delta_rule(q, k, v, beta[:, :, 0], g[:, :, 0])
    flip = lambda t: jnp.flip(t, axis=1)
    o_bwd = flip(gated_delta_rule(flip(q), flip(k), flip(v), flip(beta[:, :, 1]), flip(g[:, :, 1])))
    o = o_fwd + o_bwd
    o = o * lax.rsqrt(jnp.mean(o * o, axis=-1, keepdims=True) + EPS) * norm_dn_out.astype(jnp.float32)
    o = o * jax.nn.silu(z.astype(jnp.float32).reshape(b, s, DN_HEADS, DN_HEAD_DIM))
    return o.reshape(b, s, DN_WIDTH)


def hybrid_mixer(h, w_in, conv_w, a_log, dt_bias, norm_attn_out, norm_dn_out, w_out, rel_bias):
    b, s, _ = h.shape
    proj = h @ w_in
    cuts = [int(x) for x in np.cumsum([ATTN_WIDTH] * 3 + [DN_WIDTH] * 4 + [2 * DN_HEADS])]
    aq, ak, av, dq, dk, dv, z, beta_raw, alpha_raw = jnp.split(proj, cuts, axis=-1)
    to_heads = lambda t: t.reshape(b, s, ATTN_HEADS, ATTN_HEAD_DIM)
    q, k, v = to_heads(aq), to_heads(ak), to_heads(av)
    outs, lses = [], []
    for window, dilation in DILATION_PATTERNS:
        o_p, lse_p = dilated_window_attention(q, k, v, rel_bias, window, dilation)
        outs.append(o_p)
        lses.append(lse_p)
    mix_w = jax.nn.softmax(jnp.stack(lses, axis=0), axis=0)
    o_attn = jnp.einsum('pbsh,pbshd->bshd', mix_w, jnp.stack(outs, axis=0)).reshape(b, s, ATTN_WIDTH)
    o_attn = rms_norm(o_attn, norm_attn_out)
    o_dn = deltanet_mixer(dq, dk, dv, z, beta_raw, alpha_raw, conv_w, a_log, dt_bias, norm_dn_out)
    y = jnp.concatenate([o_attn.astype(h.dtype), o_dn.astype(h.dtype)], axis=-1)
    return y @ w_out


def encoder_layer(x, c, w_mod, b_mod, norm_ffn1, w_ffn1_in, w_ffn1_out, norm_mix, w_in, conv_w, a_log, dt_bias, norm_attn_out, norm_dn_out, w_out, norm_ffn2, w_ffn2_in, w_ffn2_out, rel_bias):
    mod = jax.nn.silu(c) @ w_mod + b_mod
    sh1, sc1, g1, sh2, sc2, g2, sh3, sc3, g3 = [m[:, None, :] for m in jnp.split(mod, N_MOD, axis=-1)]
    h = modulate(rms_norm(x, norm_ffn1), sh1, sc1)
    x = x + FFN_RESIDUAL * g1 * swiglu(h, w_ffn1_in, w_ffn1_out)
    h = modulate(rms_norm(x, norm_mix), sh2, sc2)
    x = x + g2 * hybrid_mixer(h, w_in, conv_w, a_log, dt_bias, norm_attn_out, norm_dn_out, w_out, rel_bias)
    h = modulate(rms_norm(x, norm_ffn2), sh3, sc3)
    x = x + FFN_RESIDUAL * g3 * swiglu(h, w_ffn2_in, w_ffn2_out)
    return x


def encoder_trunk(x, c, w_mod, b_mod, norm_ffn1, w_ffn1_in, w_ffn1_out, norm_mix, w_in, conv_w, a_log, dt_bias, norm_attn_out, norm_dn_out, w_out, norm_ffn2, w_ffn2_in, w_ffn2_out, rel_bias, norm_final):
    for l in range(DEPTH):
        x = encoder_layer(x, c, w_mod[l], b_mod[l], norm_ffn1[l], w_ffn1_in[l], w_ffn1_out[l], norm_mix[l], w_in[l], conv_w[l], a_log[l], dt_bias[l], norm_attn_out[l], norm_dn_out[l], w_out[l], norm_ffn2[l], w_ffn2_in[l], w_ffn2_out[l], rel_bias)
    return rms_norm(x, norm_final)


def setup_inputs(seed: int = 0) -> dict:
    key = jax.random.key(seed)
    ks = jax.random.split(key, 24)
    nrm = lambda k, shape, scale: jax.random.normal(k, shape, jnp.float32) * scale
    gain = lambda k, shape: 1.0 + 0.02 * jax.random.normal(k, shape, jnp.float32)
    dt = jnp.exp(jax.random.uniform(ks[20], (DEPTH, 2, DN_HEADS), jnp.float32, np.log(1e-3), np.log(1e-1)))
    return {
        'x_prompt': nrm(ks[0], (BATCH, SEQ, D_MODEL), 1.0),
        'x_sample': nrm(ks[1], (DEC_BATCH, DEC_SEQ, D_MODEL), 1.0),
        'c_prompt': nrm(ks[2], (BATCH, D_MODEL), 1.0),
        'c_sample': nrm(ks[3], (DEC_BATCH, D_MODEL), 1.0),
        'w_mod': nrm(ks[4], (DEPTH, D_MODEL, N_MOD * D_MODEL), D_MODEL ** -0.5),
        'b_mod': nrm(ks[5], (DEPTH, N_MOD * D_MODEL), 0.01),
        'norm_ffn1': gain(ks[6], (DEPTH, D_MODEL)),
        'w_ffn1_in': nrm(ks[7], (DEPTH, D_MODEL, 2 * D_FF), D_MODEL ** -0.5),
        'w_ffn1_out': nrm(ks[8], (DEPTH, D_FF, D_MODEL), D_FF ** -0.5),
        'norm_mix': gain(ks[9], (DEPTH, D_MODEL)),
        'w_in': nrm(ks[10], (DEPTH, D_MODEL, IN_COLS), D_MODEL ** -0.5),
        'conv_w': nrm(ks[11], (DEPTH, CONV_K, 3 * DN_WIDTH), CONV_K ** -0.5),
        'a_log': jnp.log(jax.random.uniform(ks[12], (DEPTH, 2, DN_HEADS), jnp.float32, 1.0, 16.0)),
        'dt_bias': dt + jnp.log(-jnp.expm1(-dt)),
        'norm_attn_out': gain(ks[13], (DEPTH, ATTN_WIDTH)),
        'norm_dn_out': gain(ks[14], (DEPTH, DN_HEAD_DIM)),
        'w_out': nrm(ks[15], (DEPTH, MIX_WIDTH, D_MODEL), MIX_WIDTH ** -0.5),
        'norm_ffn2': gain(ks[16], (DEPTH, D_MODEL)),
        'w_ffn2_in': nrm(ks[17], (DEPTH, D_MODEL, 2 * D_FF), D_MODEL ** -0.5),
        'w_ffn2_out': nrm(ks[18], (DEPTH, D_FF, D_MODEL), D_FF ** -0.5),
        'rel_bias': nrm(ks[19], (N_BUCKETS, ATTN_HEADS), 0.5),
        'norm_final': gain(ks[21], (D_MODEL,)),
    }


def reference(x_prompt, x_sample, c_prompt, c_sample, w_mod, b_mod, norm_ffn1, w_ffn1_in, w_ffn1_out, norm_mix, w_in, conv_w, a_log, dt_bias, norm_attn_out, norm_dn_out, w_out, norm_ffn2, w_ffn2_in, w_ffn2_out, rel_bias, norm_final):
    y_prompt = encoder_trunk(x_prompt, c_prompt, w_mod, b_mod, norm_ffn1, w_ffn1_in, w_ffn1_out, norm_mix, w_in, conv_w, a_log, dt_bias, norm_attn_out, norm_dn_out, w_out, norm_ffn2, w_ffn2_in, w_ffn2_out, rel_bias, norm_final)
    y_sample = encoder_trunk(x_sample, c_sample, w_mod, b_mod, norm_ffn1, w_ffn1_in, w_ffn1_out, norm_mix, w_in, conv_w, a_log, dt_bias, norm_attn_out, norm_dn_out, w_out, norm_ffn2, w_ffn2_in, w_ffn2_out, rel_bias, norm_final)
    return (y_prompt, y_sample)
```

```python
import functools

import numpy as np
import jax
import jax.numpy as jnp
from jax import lax
from jax.experimental import pallas as pl
from jax.experimental.pallas import tpu as pltpu

F32 = jnp.float32
BF16 = jnp.bfloat16

D_MODEL = 2048
N_HEADS = 8
HEAD_DIM = 128
HALF_WIDTH = N_HEADS * HEAD_DIM
N_MOD = 9
N_BUCKETS = 32
MAX_DISTANCE = 1024
CONV_K = 5
CHUNK = 64
SIDE = 64
ATT_TILE = 1024
EPS = 1e-6
NEG_INF = -1e30
FFN_RESIDUAL = 0.5

LANES = 128
VMEM_LIMIT = 56 * 1024 * 1024

ROW_TILE = 512
FF_TILE = 512
MOD_TILE = 1024
DN_TILE = 256
SCAN_TILE = 512


def _cparams(sem):
    return pltpu.CompilerParams(dimension_semantics=sem, vmem_limit_bytes=VMEM_LIMIT)


def _rms(x, w):
    return x * lax.rsqrt(jnp.mean(x * x, axis=-1, keepdims=True) + EPS) * w


def _silu(x):
    return x * jax.nn.sigmoid(x)


def _mod_kernel(c_ref, w_ref, b_ref, o_ref):
    a = _silu(c_ref[...])
    o_ref[...] = jnp.dot(a, w_ref[...], precision=lax.Precision.HIGHEST,
                         preferred_element_type=F32) + b_ref[...]


def _modulation(c, w_mod, b_mod):
    rows = c.shape[0]
    n = w_mod.shape[1]
    return pl.pallas_call(
        _mod_kernel,
        grid=(n // MOD_TILE,),
        in_specs=[pl.BlockSpec((rows, D_MODEL), lambda j: (0, 0)),
                  pl.BlockSpec((D_MODEL, MOD_TILE), lambda j: (0, j)),
                  pl.BlockSpec((1, MOD_TILE), lambda j: (0, j))],
        out_specs=pl.BlockSpec((rows, MOD_TILE), lambda j: (0, j)),
        out_shape=jax.ShapeDtypeStruct((rows, n), F32),
        compiler_params=_cparams(("parallel",)),
        name="modulation",
    )(c, w_mod, b_mod.reshape(1, n))


def _ffn_kernel(x_ref, mod_ref, nw_ref, wg_ref, wu_ref, wo_ref, fw_ref, o_ref,
                h_scr, acc_scr, *, sub, final):
    j = pl.program_id(2)

    @pl.when(j == 0)
    def _():
        shift = mod_ref[3 * sub:3 * sub + 1, :]
        scale = mod_ref[3 * sub + 1:3 * sub + 2, :]
        h = _rms(x_ref[...], nw_ref[...]) * (1.0 + scale) + shift
        h_scr[...] = h.astype(BF16)
        acc_scr[...] = jnp.zeros_like(acc_scr)

    h = h_scr[...]
    g = jnp.dot(h, wg_ref[...], preferred_element_type=F32)
    u = jnp.dot(h, wu_ref[...], preferred_element_type=F32)
    a = (_silu(g) * u).astype(BF16)
    acc_scr[...] += jnp.dot(a, wo_ref[...], preferred_element_type=F32)

    @pl.when(j == pl.num_programs(2) - 1)
    def _():
        gate = mod_ref[3 * sub + 2:3 * sub + 3, :]
        out = x_ref[...] + FFN_RESIDUAL * gate * acc_scr[...]
        if final:
            out = _rms(out, fw_ref[...])
        o_ref[...] = out


def _ffn(x, mod, norm_w, w_in, w_out, final_w, *, sub, final):
    b, s, d = x.shape
    d_ff = w_out.shape[0]
    nf = d_ff // FF_TILE
    tm = ROW_TILE
    kern = functools.partial(_ffn_kernel, sub=sub, final=final)
    return pl.pallas_call(
        kern,
        grid=(b, s // tm, nf),
        in_specs=[pl.BlockSpec((None, tm, d), lambda bi, i, j: (bi, i, 0)),
                  pl.BlockSpec((None, N_MOD, d), lambda bi, i, j: (bi, 0, 0)),
                  pl.BlockSpec((1, d), lambda bi, i, j: (0, 0)),
                  pl.BlockSpec((d, FF_TILE), lambda bi, i, j: (0, j)),
                  pl.BlockSpec((d, FF_TILE), lambda bi, i, j: (0, j + nf)),
                  pl.BlockSpec((FF_TILE, d), lambda bi, i, j: (j, 0)),
                  pl.BlockSpec((1, d), lambda bi, i, j: (0, 0))],
        out_specs=pl.BlockSpec((None, tm, d), lambda bi, i, j: (bi, i, 0)),
        out_shape=jax.ShapeDtypeStruct((b, s, d), F32),
        scratch_shapes=[pltpu.VMEM((tm, d), BF16), pltpu.VMEM((tm, d), F32)],
        compiler_params=_cparams(("parallel", "parallel", "arbitrary")),
        name="ffn%d" % sub,
    )(x, mod, norm_w.reshape(1, d), w_in, w_in, w_out, final_w.reshape(1, d))


def _inproj_kernel(x_ref, mod_ref, nw_ref, w_ref, wg_ref,
                   aq_ref, ak_ref, av_ref, dq_ref, dk_ref, dv_ref, z_ref, gt_ref, h_scr):
    j = pl.program_id(2)

    @pl.when(j == 0)
    def _():
        shift = mod_ref[3:4, :]
        scale = mod_ref[4:5, :]
        h = _rms(x_ref[...], nw_ref[...]) * (1.0 + scale) + shift
        h_scr[...] = h.astype(BF16)
        gt_ref[...] = jnp.dot(h_scr[...], wg_ref[...], preferred_element_type=F32)

    y = jnp.dot(h_scr[...], w_ref[...], preferred_element_type=F32)

    def write_heads(ref):
        for hd in range(N_HEADS):
            ref[hd] = y[:, hd * HEAD_DIM:(hd + 1) * HEAD_DIM].astype(BF16)

    def write_flat(ref):
        ref[...] = y

    for idx, ref in enumerate((aq_ref, ak_ref, av_ref)):
        pl.when(j == idx)(functools.partial(write_heads, ref))
    for idx, ref in enumerate((dq_ref, dk_ref, dv_ref, z_ref)):
        pl.when(j == idx + 3)(functools.partial(write_flat, ref))


def _inproj(x, mod, norm_w, w_main, w_gate):
    b, s, d = x.shape
    tm = ROW_TILE
    hw = HALF_WIDTH
    head_spec = pl.BlockSpec((None, N_HEADS, tm, HEAD_DIM), lambda bi, i, j: (bi, 0, i, 0))
    flat_spec = pl.BlockSpec((None, tm, hw), lambda bi, i, j: (bi, i, 0))
    head_shape = jax.ShapeDtypeStruct((b, N_HEADS, s, HEAD_DIM), BF16)
    flat_shape = jax.ShapeDtypeStruct((b, s, hw), F32)
    return pl.pallas_call(
        _inproj_kernel,
        grid=(b, s // tm, 7),
        in_specs=[pl.BlockSpec((None, tm, d), lambda bi, i, j: (bi, i, 0)),
                  pl.BlockSpec((None, N_MOD, d), lambda bi, i, j: (bi, 0, 0)),
                  pl.BlockSpec((1, d), lambda bi, i, j: (0, 0)),
                  pl.BlockSpec((d, hw), lambda bi, i, j: (0, j)),
                  pl.BlockSpec((d, LANES), lambda bi, i, j: (0, 0))],
        out_specs=[head_spec] * 3 + [flat_spec] * 4
                  + [pl.BlockSpec((None, tm, LANES), lambda bi, i, j: (bi, i, 0))],
        out_shape=[head_shape] * 3 + [flat_shape] * 4
                  + [jax.ShapeDtypeStruct((b, s, LANES), F32)],
        scratch_shapes=[pltpu.VMEM((tm, d), BF16)],
        compiler_params=_cparams(("parallel", "parallel", "arbitrary")),
        name="inproj",
    )(x, mod, norm_w.reshape(1, d), w_main, w_gate)


def _t5_bucket(rel):
    half = N_BUCKETS // 2
    max_exact = half // 2
    n = np.abs(rel)
    large = max_exact + (np.log(np.maximum(n, 1) / max_exact) / np.log(MAX_DISTANCE / max_exact)
                         * (half - max_exact)).astype(np.int32)
    large = np.minimum(large, half - 1)
    return (np.where(rel > 0, half, 0) + np.where(n < max_exact, n, large)).astype(np.int32)


def _attn_bias_tables(rel_bias):
    rel16 = np.arange(3 * SIDE)[None, :] - SIDE - np.arange(SIDE)[:, None]
    perm4 = np.array([4 * (i % 16) + i // 16 for i in range(SIDE)])
    col = np.arange(3 * SIDE)
    rel4 = (SIDE * (col // SIDE - 1) + perm4[col % SIDE])[None, :] - perm4[:, None]
    perm1 = np.array([16 * (i % 16) + i // 16 for i in range(256)])
    col = np.arange(768)
    rel1 = (256 * (col // 256 - 1) + perm1[col % 256])[None, :] - perm1[:, None]
    tables = []
    for rel, dil in ((rel1, 1), (rel4, 4), (rel16, 16)):
        allowed = np.abs(rel) <= SIDE
        bias = jnp.transpose(rel_bias[_t5_bucket(rel * dil)], (2, 0, 1)).astype(F32)
        tables.append(jnp.where(jnp.asarray(allowed)[None], bias, NEG_INF))
    return tables


def _attn_kernel(q_ref, kp_ref, kc_ref, kn_ref, vp_ref, vc_ref, vn_ref,
                 b1_ref, b4_ref, b16_ref, o_ref,
                 o1_scr, l1_scr, o4_scr, l4_scr, o16_scr, l16_scr):
    t = pl.program_id(2)
    neg_prev = jnp.where(t > 0, 0.0, NEG_INF).astype(F32)
    neg_next = jnp.where(t < pl.num_programs(2) - 1, 0.0, NEG_INF).astype(F32)
    scale = HEAD_DIM ** -0.5

    def lanes(r):
        return slice(r * HEAD_DIM, (r + 1) * HEAD_DIM)

    def attend(q, keys, vals, bias, edge_prev, edge_next):
        nk = keys.shape[0]
        s = lax.dot_general(q, keys, (((1,), (1,)), ((), ())), preferred_element_type=F32)
        s = s * scale + bias
        if edge_prev or edge_next:
            col = lax.broadcasted_iota(jnp.int32, (1, nk), 1)
            edge = jnp.zeros((1, nk), F32)
            if edge_prev:
                edge = jnp.where(col < nk // 3, neg_prev, edge)
            if edge_next:
                edge = jnp.where(col >= 2 * (nk // 3), neg_next, edge)
            s = s + edge
        m = jnp.max(s, axis=-1, keepdims=True)
        p = jnp.exp(s - m)
        den = jnp.sum(p, axis=-1, keepdims=True)
        o = jnp.dot(p.astype(BF16), vals, preferred_element_type=F32) / den
        lse = m + jnp.log(den)
        return o, jnp.broadcast_to(lse, o.shape)

    b16 = b16_ref[...]
    for r in range(16):
        keys = jnp.concatenate([kp_ref[:, lanes(r)], kc_ref[:, lanes(r)], kn_ref[:, lanes(r)]], axis=0)
        vals = jnp.concatenate([vp_ref[:, lanes(r)], vc_ref[:, lanes(r)], vn_ref[:, lanes(r)]], axis=0)
        o, l = attend(q_ref[:, lanes(r)], keys, vals, b16, True, True)
        o16_scr[:, lanes(r)] = o
        l16_scr[:, lanes(r)] = l

    def blk4(ref, r4, m):
        return jnp.concatenate([ref[16 * m:16 * m + 16, lanes(r4 + 4 * c)] for c in range(4)], axis=0)

    b4 = b4_ref[...]
    for r4 in range(4):
        for m in range(4):
            def nbr(pref, cref, nref):
                lo = blk4(pref, r4, 3) if m == 0 else blk4(cref, r4, m - 1)
                hi = blk4(nref, r4, 0) if m == 3 else blk4(cref, r4, m + 1)
                return jnp.concatenate([lo, blk4(cref, r4, m), hi], axis=0)
            o, l = attend(blk4(q_ref, r4, m), nbr(kp_ref, kc_ref, kn_ref), nbr(vp_ref, vc_ref, vn_ref),
                          b4, m == 0, m == 3)
            for c in range(4):
                o4_scr[16 * m:16 * m + 16, lanes(r4 + 4 * c)] = o[16 * c:16 * c + 16]
                l4_scr[16 * m:16 * m + 16, lanes(r4 + 4 * c)] = l[16 * c:16 * c + 16]

    def grp1(ref, g):
        return jnp.concatenate([ref[16 * g:16 * g + 16, lanes(r)] for r in range(16)], axis=0)

    b1 = b1_ref[...]
    for g in range(4):
        def nbr1(pref, cref, nref):
            lo = grp1(pref, 3) if g == 0 else grp1(cref, g - 1)
            hi = grp1(nref, 0) if g == 3 else grp1(cref, g + 1)
            return jnp.concatenate([lo, grp1(cref, g), hi], axis=0)
        o, l = attend(grp1(q_ref, g), nbr1(kp_ref, kc_ref, kn_ref), nbr1(vp_ref, vc_ref, vn_ref),
                      b1, g == 0, g == 3)
        for r in range(16):
            o1_scr[16 * g:16 * g + 16, lanes(r)] = o[16 * r:16 * r + 16]
            l1_scr[16 * g:16 * g + 16, lanes(r)] = l[16 * r:16 * r + 16]

    l1, l4, l16 = l1_scr[...], l4_scr[...], l16_scr[...]
    mx = jnp.maximum(jnp.maximum(l1, l4), l16)
    w1, w4, w16 = jnp.exp(l1 - mx), jnp.exp(l4 - mx), jnp.exp(l16 - mx)
    num = w1 * o1_scr[...] + w4 * o4_scr[...] + w16 * o16_scr[...]
    o_ref[...] = num / (w1 + w4 + w16)


def _attention(q, k, v, tables):
    b, nh, s, dh = q.shape
    nt = s // ATT_TILE
    rows, width = ATT_TILE // 16, 16 * dh
    view = lambda a: a.reshape(b, nh, nt, rows, width)
    cur = pl.BlockSpec((None, None, None, rows, width), lambda bi, h, t: (bi, h, t, 0, 0))
    prev = pl.BlockSpec((None, None, None, rows, width),
                        lambda bi, h, t: (bi, h, jnp.maximum(t - 1, 0), 0, 0))
    nxt = pl.BlockSpec((None, None, None, rows, width),
                       lambda bi, h, t: (bi, h, jnp.minimum(t + 1, nt - 1), 0, 0))
    b1, b4, b16 = tables
    tspec = lambda a: pl.BlockSpec((None,) + a.shape[1:], lambda bi, h, t: (h, 0, 0))
    out = pl.pallas_call(
        _attn_kernel,
        grid=(b, nh, nt),
        in_specs=[cur, prev, cur, nxt, prev, cur, nxt, tspec(b1), tspec(b4), tspec(b16)],
        out_specs=cur,
        out_shape=jax.ShapeDtypeStruct((b, nh, nt, rows, width), F32),
        scratch_shapes=[pltpu.VMEM((rows, width), F32)] * 6,
        compiler_params=_cparams(("parallel", "parallel", "parallel")),
        name="dilated_attention",
    )(view(q), view(k), view(k), view(k), view(v), view(v), view(v), b1, b4, b16)
    return out.reshape(b, nh, s, dh)


def _dn_prep_kernel(qp_ref, qc_ref, qn_ref, kp_ref, kc_ref, kn_ref, vp_ref, vc_ref, vn_ref,
                    gt_ref, cw_ref, alog_ref, dtb_ref,
                    uf_ref, wf_ref, qf_ref, kf_ref, af_ref,
                    ub_ref, wb_ref, qb_ref, kb_ref, ab_ref, gl_ref):
    i = pl.program_id(1)
    n = DN_TILE
    has_prev = (i > 0).astype(F32)
    has_next = (i < pl.num_programs(1) - 1).astype(F32)
    pad = CONV_K // 2

    def conv_silu(p_ref, c_ref, n_ref, part):
        xt = jnp.concatenate([p_ref[8 - pad:8, :] * has_prev, c_ref[...], n_ref[0:pad, :] * has_next], axis=0)
        acc = jnp.zeros((n, HALF_WIDTH), F32)
        for tap in range(CONV_K):
            w = cw_ref[tap:tap + 1, part * HALF_WIDTH:(part + 1) * HALF_WIDTH]
            acc = acc + xt[tap:tap + n, :] * w
        return _silu(acc)

    q_all = conv_silu(qp_ref, qc_ref, qn_ref, 0)
    k_all = conv_silu(kp_ref, kc_ref, kn_ref, 1)
    v_all = conv_silu(vp_ref, vc_ref, vn_ref, 2)

    gates = gt_ref[...]
    beta_all = jax.nn.sigmoid(gates)
    xa = gates + dtb_ref[...]
    softplus = jnp.maximum(xa, 0.0) + jnp.log1p(jnp.exp(-jnp.abs(xa)))
    g_all = -jnp.exp(alog_ref[...]) * softplus

    row = lax.broadcasted_iota(jnp.int32, (n, n), 0)
    col = lax.broadcasted_iota(jnp.int32, (n, n), 1)
    same = (row // CHUNK) == (col // CHUNK)
    hp = lax.Precision.HIGHEST
    tri_f = jnp.where(same & (col <= row), 1.0, 0.0).astype(F32)
    tri_b = jnp.where(same & (col >= row), 1.0, 0.0).astype(F32)
    gc_f = jnp.dot(tri_f, g_all, precision=hp, preferred_element_type=F32)
    gc_b = jnp.dot(tri_b, g_all, precision=hp, preferred_element_type=F32)
    gtot = jnp.dot(jnp.where(same, 1.0, 0.0).astype(F32), g_all, precision=hp, preferred_element_type=F32)
    gl_ref[...] = jnp.exp(gtot)
    gc_t = (gc_f.T, gc_b.T)
    gc = (gc_f, gc_b)
    incl = (same & (col <= row), same & (col >= row))
    offdiag = row != col
    eye = jnp.where(offdiag, 0.0, 1.0).astype(F32)
    level = [((row // (2 * sz)) == (col // (2 * sz))) & ((row // sz) != (col // sz))
             for sz in (1, 2, 4, 8, 16, 32)]
    outs = ((uf_ref, wf_ref, qf_ref, kf_ref, af_ref), (ub_ref, wb_ref, qb_ref, kb_ref, ab_ref))
    qscale = HEAD_DIM ** -0.5

    for hd in range(N_HEADS):
        hl = slice(hd * HEAD_DIM, (hd + 1) * HEAD_DIM)
        q = q_all[:, hl]
        k = k_all[:, hl]
        v = v_all[:, hl]
        q = q * lax.rsqrt(jnp.sum(q * q, axis=-1, keepdims=True) + EPS) * qscale
        k = k * lax.rsqrt(jnp.sum(k * k, axis=-1, keepdims=True) + EPS)
        kb16 = k.astype(BF16)
        nt_dims = (((1,), (1,)), ((), ()))
        kk = lax.dot_general(kb16, kb16, nt_dims, preferred_element_type=F32)
        qk = lax.dot_general(q.astype(BF16), kb16, nt_dims, preferred_element_type=F32)
        for d in range(2):
            cb = d * N_HEADS + hd
            cg = 2 * N_HEADS + cb
            beta = beta_all[:, cb:cb + 1]
            gcc = gc[d][:, cg:cg + 1]
            gcr = gc_t[d][cg:cg + 1, :]
            decay = jnp.exp(jnp.where(incl[d], gcc - gcr, NEG_INF))
            lmat = jnp.where(offdiag, beta * kk * decay, 0.0)
            tinv = eye - jnp.where(level[0], lmat, 0.0)
            for lv in range(1, 6):
                tb = tinv.astype(BF16)
                x = jnp.dot(jnp.where(level[lv], lmat, 0.0).astype(BF16), tb, preferred_element_type=F32)
                tinv = tinv - jnp.dot(tb, x.astype(BF16), preferred_element_type=F32)
            egc = jnp.exp(gcc)
            rhs = jnp.concatenate([v * beta, k * (beta * egc)], axis=1).astype(BF16)
            sol = jnp.dot(tinv.astype(BF16), rhs, preferred_element_type=F32)
            u_ref, w_ref, qd_ref, kd_ref, a_ref = outs[d]
            u_ref[:, hl] = sol[:, :HEAD_DIM]
            w_ref[:, hl] = sol[:, HEAD_DIM:].astype(BF16)
            qd_ref[:, hl] = (q * egc).astype(BF16)
            kd_ref[:, hl] = (k * jnp.exp(gtot[:, cg:cg + 1] - gcc)).astype(BF16)
            aqk = (qk * decay).astype(BF16)
            for c in range(n // CHUNK):
                a_ref[c * CHUNK:(c + 1) * CHUNK, hd * CHUNK:(hd + 1) * CHUNK] = \
                    aqk[c * CHUNK:(c + 1) * CHUNK, c * CHUNK:(c + 1) * CHUNK]


def _dn_prep(dq, dk, dv, gates, conv_w, alog_row, dtb_row):
    b, s, hw = dq.shape
    n = DN_TILE
    nb = s // n
    r8 = n // 8
    cur = pl.BlockSpec((None, n, hw), lambda bi, i: (bi, i, 0))
    prev = pl.BlockSpec((None, 8, hw), lambda bi, i: (bi, jnp.maximum(i * r8 - 1, 0), 0))
    nxt = pl.BlockSpec((None, 8, hw), lambda bi, i: (bi, jnp.minimum((i + 1) * r8, s // 8 - 1), 0))
    small = pl.BlockSpec((None, n, LANES), lambda bi, i: (bi, i, 0))
    aspec = pl.BlockSpec((None, n, N_HEADS * CHUNK), lambda bi, i: (bi, i, 0))
    const = lambda a: pl.BlockSpec(a.shape, lambda bi, i: (0, 0))
    f32s = jax.ShapeDtypeStruct((b, s, hw), F32)
    b16s = jax.ShapeDtypeStruct((b, s, hw), BF16)
    a16s = jax.ShapeDtypeStruct((b, s, N_HEADS * CHUNK), BF16)
    per_dir_specs = [cur, cur, cur, cur, aspec]
    per_dir_shapes = [f32s, b16s, b16s, b16s, a16s]
    return pl.pallas_call(
        _dn_prep_kernel,
        grid=(b, nb),
        in_specs=[prev, cur, nxt] * 3 + [small, const(conv_w), const(alog_row), const(dtb_row)],
        out_specs=per_dir_specs * 2 + [small],
        out_shape=per_dir_shapes * 2 + [jax.ShapeDtypeStruct((b, s, LANES), F32)],
        compiler_params=_cparams(("parallel", "parallel")),
        name="deltanet_prep",
    )(dq, dq, dq, dk, dk, dk, dv, dv, dv, gates, conv_w, alog_row, dtb_row)


def _dn_scan_kernel(uf_ref, wf_ref, qf_ref, kf_ref, af_ref, glf_ref,
                    ub_ref, wb_ref, qb_ref, kb_ref, ab_ref, glb_ref,
                    of_ref, ob_ref, state_scr):
    @pl.when(pl.program_id(1) == 0)
    def _():
        state_scr[...] = jnp.zeros_like(state_scr)

    ins = ((uf_ref, wf_ref, qf_ref, kf_ref, af_ref, glf_ref, of_ref),
           (ub_ref, wb_ref, qb_ref, kb_ref, ab_ref, glb_ref, ob_ref))
    nchunks = SCAN_TILE // CHUNK

    def body(ci, carry):
        for d in range(2):
            u_ref, w_ref, qd_ref, kd_ref, a_ref, gl_ref, o_ref = ins[d]
            cidx = ci if d == 0 else nchunks - 1 - ci
            r0 = pl.multiple_of(cidx * CHUNK, CHUNK)
            rows = pl.ds(r0, CHUNK)
            gl_row = gl_ref[pl.ds(r0, 1), :]
            for hd in range(N_HEADS):
                hl = slice(hd * HEAD_DIM, (hd + 1) * HEAD_DIM)
                cg = 2 * N_HEADS + d * N_HEADS + hd
                state = state_scr[d, hd]
                sb = state.astype(BF16)
                wq = jnp.concatenate([w_ref[rows, hl], qd_ref[rows, hl]], axis=0)
                ws = jnp.dot(wq, sb, preferred_element_type=F32)
                v_new = u_ref[rows, hl] - ws[:CHUNK]
                vb = v_new.astype(BF16)
                a = a_ref[rows, hd * CHUNK:(hd + 1) * CHUNK]
                o_ref[rows, hl] = ws[CHUNK:] + jnp.dot(a, vb, preferred_element_type=F32)
                upd = lax.dot_general(kd_ref[rows, hl], vb, (((0,), (0,)), ((), ())),
                                      preferred_element_type=F32)
                state_scr[d, hd] = state * gl_row[:, cg:cg + 1] + upd
        return carry

    lax.fori_loop(0, nchunks, body, 0)


def _dn_scan(prep):
    uf, wf, qf, kf, af, ub, wb, qb, kb, ab, gl = prep
    b, s, hw = uf.shape
    n = SCAN_TILE
    nb = s // n
    fwd = lambda w: pl.BlockSpec((None, n, w), lambda bi, i: (bi, i, 0))
    bwd = lambda w: pl.BlockSpec((None, n, w), lambda bi, i: (bi, nb - 1 - i, 0))
    widths = [hw, hw, hw, hw, N_HEADS * CHUNK, LANES]
    return pl.pallas_call(
        _dn_scan_kernel,
        grid=(b, nb),
        in_specs=[fwd(w) for w in widths] + [bwd(w) for w in widths],
        out_specs=[fwd(hw), bwd(hw)],
        out_shape=[jax.ShapeDtypeStruct((b, s, hw), F32)] * 2,
        scratch_shapes=[pltpu.VMEM((2, N_HEADS, HEAD_DIM, HEAD_DIM), F32)],
        compiler_params=_cparams(("parallel", "arbitrary")),
        name="deltanet_scan",
    )(uf, wf, qf, kf, af, gl, ub, wb, qb, kb, ab, gl)


def _outproj_kernel(x_ref, mod_ref, oa_ref, of_ref, ob_ref, z_ref, na_ref, nd_ref, w_ref, o_ref):
    oa = jnp.concatenate([oa_ref[hd] for hd in range(N_HEADS)], axis=-1)
    oa = _rms(oa, na_ref[...])
    od = of_ref[...] + ob_ref[...]
    z = z_ref[...]
    parts = []
    for hd in range(N_HEADS):
        hl = slice(hd * HEAD_DIM, (hd + 1) * HEAD_DIM)
        parts.append(_rms(od[:, hl], nd_ref[...]) * _silu(z[:, hl]))
    y = jnp.concatenate([oa] + parts, axis=-1).astype(BF16)
    out = jnp.dot(y, w_ref[...], preferred_element_type=F32)
    o_ref[...] = x_ref[...] + mod_ref[5:6, :] * out


def _outproj(x, mod, o_attn, o_f, o_b, z, norm_attn, norm_dn, w_out):
    b, s, d = x.shape
    tm = ROW_TILE
    hw = HALF_WIDTH
    flat = pl.BlockSpec((None, tm, hw), lambda bi, i: (bi, i, 0))
    return pl.pallas_call(
        _outproj_kernel,
        grid=(b, s // tm),
        in_specs=[pl.BlockSpec((None, tm, d), lambda bi, i: (bi, i, 0)),
                  pl.BlockSpec((None, N_MOD, d), lambda bi, i: (bi, 0, 0)),
                  pl.BlockSpec((None, N_HEADS, tm, HEAD_DIM), lambda bi, i: (bi, 0, i, 0)),
                  flat, flat, flat,
                  pl.BlockSpec((1, hw), lambda bi, i: (0, 0)),
                  pl.BlockSpec((1, HEAD_DIM), lambda bi, i: (0, 0)),
                  pl.BlockSpec((d, d), lambda bi, i: (0, 0))],
        out_specs=pl.BlockSpec((None, tm, d), lambda bi, i: (bi, i, 0)),
        out_shape=jax.ShapeDtypeStruct((b, s, d), F32),
        compiler_params=_cparams(("parallel", "parallel")),
        name="outproj",
    )(x, mod, o_attn, o_f, o_b, z, norm_attn.reshape(1, hw), norm_dn.reshape(1, HEAD_DIM), w_out)


def _pad_lanes(vec, offset):
    flat = vec.reshape(-1).astype(F32)
    return jnp.zeros((1, LANES), F32).at[0, offset:offset + flat.shape[0]].set(flat)


def _trunk(x, mod, p):
    x = _ffn(x, mod, p["norm_ffn1"], p["w_ffn1_in"], p["w_ffn1_out"], p["norm_final"], sub=0, final=False)
    aq, ak, av, dq, dk, dv, z, gates = _inproj(x, mod, p["norm_mix"], p["w_in_main"], p["w_in_gate"])
    o_attn = _attention(aq, ak, av, p["bias_tables"])
    prep = _dn_prep(dq, dk, dv, gates, p["conv_w"], p["alog_row"], p["dtb_row"])
    o_f, o_b = _dn_scan(prep)
    x = _outproj(x, mod, o_attn, o_f, o_b, z, p["norm_attn_out"], p["norm_dn_out"], p["w_out"])
    return _ffn(x, mod, p["norm_ffn2"], p["w_ffn2_in"], p["w_ffn2_out"], p["norm_final"], sub=2, final=True)


def kernel(x_prompt, x_sample, c_prompt, c_sample, w_mod, b_mod, norm_ffn1, w_ffn1_in, w_ffn1_out, norm_mix, w_in, conv_w, a_log, dt_bias, norm_attn_out, norm_dn_out, w_out, norm_ffn2, w_ffn2_in, w_ffn2_out, rel_bias, norm_final):
    assert w_mod.shape[0] == 1, "one encoder layer"
    bp, bs = c_prompt.shape[0], c_sample.shape[0]
    n_gate = 4 * N_HEADS
    cut = 3 * HALF_WIDTH + 4 * HALF_WIDTH
    p = {
        "norm_ffn1": norm_ffn1[0], "w_ffn1_in": w_ffn1_in[0].astype(BF16), "w_ffn1_out": w_ffn1_out[0].astype(BF16),
        "norm_mix": norm_mix[0],
        "w_in_main": w_in[0, :, :cut].astype(BF16),
        "w_in_gate": jnp.pad(w_in[0, :, cut:], ((0, 0), (0, LANES - n_gate))).astype(BF16),
        "conv_w": conv_w[0],
        "alog_row": _pad_lanes(a_log[0], 2 * N_HEADS), "dtb_row": _pad_lanes(dt_bias[0], 2 * N_HEADS),
        "norm_attn_out": norm_attn_out[0], "norm_dn_out": norm_dn_out[0], "w_out": w_out[0].astype(BF16),
        "norm_ffn2": norm_ffn2[0], "w_ffn2_in": w_ffn2_in[0].astype(BF16), "w_ffn2_out": w_ffn2_out[0].astype(BF16),
        "norm_final": norm_final,
        "bias_tables": _attn_bias_tables(rel_bias),
    }
    c_all = jnp.concatenate([c_prompt, c_sample, jnp.zeros((8 - (bp + bs) % 8, D_MODEL), F32)], axis=0)
    mod = _modulation(c_all, w_mod[0], b_mod[0]).reshape(c_all.shape[0], N_MOD, D_MODEL)
    y_prompt = _trunk(x_prompt, mod[:bp], p)
    y_sample = _trunk(x_sample, mod[bp:bp + bs], p)
    return (y_prompt, y_sample)
```

```python
import functools

import numpy as np
import jax
import jax.numpy as jnp
from jax import lax
from jax.experimental import pallas as pl
from jax.experimental.pallas import tpu as pltpu

F32 = jnp.float32
BF16 = jnp.bfloat16

D_MODEL = 2048
N_HEADS = 8
HEAD_DIM = 128
HALF_WIDTH = N_HEADS * HEAD_DIM
N_MOD = 9
N_BUCKETS = 32
MAX_DISTANCE = 1024
CONV_K = 5
CHUNK = 64
SIDE = 64
ATT_TILE = 1024
EPS = 1e-6
NEG_INF = -1e30
FFN_RESIDUAL = 0.5

LANES = 128
VMEM_LIMIT = 56 * 1024 * 1024

ROW_TILE = 512
FF_TILE = 512
MOD_TILE = 1024
DN_TILE = 256
SCAN_TILE = 512


def _cparams(sem):
    return pltpu.CompilerParams(dimension_semantics=sem, vmem_limit_bytes=VMEM_LIMIT)


def _rms(x, w):
    return x * lax.rsqrt(jnp.mean(x * x, axis=-1, keepdims=True) + EPS) * w


def _silu(x):
    return x * jax.nn.sigmoid(x)


def _mod_kernel(c_ref, w_ref, b_ref, o_ref):
    a = _silu(c_ref[...])
    o_ref[...] = jnp.dot(a, w_ref[...], precision=lax.Precision.HIGHEST,
                         preferred_element_type=F32) + b_ref[...]


def _modulation(c, w_mod, b_mod):
    rows = c.shape[0]
    n = w_mod.shape[1]
    return pl.pallas_call(
        _mod_kernel,
        grid=(n // MOD_TILE,),
        in_specs=[pl.BlockSpec((rows, D_MODEL), lambda j: (0, 0)),
                  pl.BlockSpec((D_MODEL, MOD_TILE), lambda j: (0, j)),
                  pl.BlockSpec((1, MOD_TILE), lambda j: (0, j))],
        out_specs=pl.BlockSpec((rows, MOD_TILE), lambda j: (0, j)),
        out_shape=jax.ShapeDtypeStruct((rows, n), F32),
        compiler_params=_cparams(("parallel",)),
        name="modulation",
    )(c, w_mod, b_mod.reshape(1, n))


def _ffn_kernel(x_ref, mod_ref, nw_ref, wg_ref, wu_ref, wo_ref, fw_ref, o_ref,
                h_scr, acc_scr, *, sub, final):
    j = pl.program_id(2)

    @pl.when(j == 0)
    def _():
        shift = mod_ref[3 * sub:3 * sub + 1, :]
        scale = mod_ref[3 * sub + 1:3 * sub + 2, :]
        h = _rms(x_ref[...], nw_ref[...]) * (1.0 + scale) + shift
        h_scr[...] = h.astype(BF16)
        acc_scr[...] = jnp.zeros_like(acc_scr)

    h = h_scr[...]
    g = jnp.dot(h, wg_ref[...], preferred_element_type=F32)
    u = jnp.dot(h, wu_ref[...], preferred_element_type=F32)
    a = (_silu(g) * u).astype(BF16)
    acc_scr[...] += jnp.dot(a, wo_ref[...], preferred_element_type=F32)

    @pl.when(j == pl.num_programs(2) - 1)
    def _():
        gate = mod_ref[3 * sub + 2:3 * sub + 3, :]
        out = x_ref[...] + FFN_RESIDUAL * gate * acc_scr[...]
        if final:
            out = _rms(out, fw_ref[...])
        o_ref[...] = out


def _ffn(x, mod, norm_w, w_in, w_out, final_w, *, sub, final):
    b, s, d = x.shape
    d_ff = w_out.shape[0]
    nf = d_ff // FF_TILE
    tm = ROW_TILE
    kern = functools.partial(_ffn_kernel, sub=sub, final=final)
    return pl.pallas_call(
        kern,
        grid=(b, s // tm, nf),
        in_specs=[pl.BlockSpec((None, tm, d), lambda bi, i, j: (bi, i, 0)),
                  pl.BlockSpec((None, N_MOD, d), lambda bi, i, j: (bi, 0, 0)),
                  pl.BlockSpec((1, d), lambda bi, i, j: (0, 0)),
                  pl.BlockSpec((d, FF_TILE), lambda bi, i, j: (0, j)),
                  pl.BlockSpec((d, FF_TILE), lambda bi, i, j: (0, j + nf)),
                  pl.BlockSpec((FF_TILE, d), lambda bi, i, j: (j, 0)),
                  pl.BlockSpec((1, d), lambda bi, i, j: (0, 0))],
        out_specs=pl.BlockSpec((None, tm, d), lambda bi, i, j: (bi, i, 0)),
        out_shape=jax.ShapeDtypeStruct((b, s, d), F32),
        scratch_shapes=[pltpu.VMEM((tm, d), BF16), pltpu.VMEM((tm, d), F32)],
        compiler_params=_cparams(("parallel", "parallel", "arbitrary")),
        name="ffn%d" % sub,
    )(x, mod, norm_w.reshape(1, d), w_in, w_in, w_out, final_w.reshape(1, d))


def _inproj_kernel(x_ref, mod_ref, nw_ref, w_ref, wg_ref,
                   aq_ref, ak_ref, av_ref, dq_ref, dk_ref, dv_ref, z_ref, gt_ref, h_scr):
    j = pl.program_id(2)

    @pl.when(j == 0)
    def _():
        shift = mod_ref[3:4, :]
        scale = mod_ref[4:5, :]
        h = _rms(x_ref[...], nw_ref[...]) * (1.0 + scale) + shift
        h_scr[...] = h.astype(BF16)
        gt_ref[...] = jnp.dot(h_scr[...], wg_ref[...], preferred_element_type=F32)

    y = jnp.dot(h_scr[...], w_ref[...], preferred_element_type=F32)

    def write_heads(ref):
        for hd in range(N_HEADS):
            ref[hd] = y[:, hd * HEAD_DIM:(hd + 1) * HEAD_DIM].astype(BF16)

    def write_flat(ref):
        ref[...] = y

    for idx, ref in enumerate((aq_ref, ak_ref, av_ref)):
        pl.when(j == idx)(functools.partial(write_heads, ref))
    for idx, ref in enumerate((dq_ref, dk_ref, dv_ref, z_ref)):
        pl.when(j == idx + 3)(functools.partial(write_flat, ref))


def _inproj(x, mod, norm_w, w_main, w_gate):
    b, s, d = x.shape
    tm = ROW_TILE
    hw = HALF_WIDTH
    head_spec = pl.BlockSpec((None, N_HEADS, tm, HEAD_DIM), lambda bi, i, j: (bi, 0, i, 0))
    flat_spec = pl.BlockSpec((None, tm, hw), lambda bi, i, j: (bi, i, 0))
    head_shape = jax.ShapeDtypeStruct((b, N_HEADS, s, HEAD_DIM), BF16)
    flat_shape = jax.ShapeDtypeStruct((b, s, hw), F32)
    return pl.pallas_call(
        _inproj_kernel,
        grid=(b, s // tm, 7),
        in_specs=[pl.BlockSpec((None, tm, d), lambda bi, i, j: (bi, i, 0)),
                  pl.BlockSpec((None, N_MOD, d), lambda bi, i, j: (bi, 0, 0)),
                  pl.BlockSpec((1, d), lambda bi, i, j: (0, 0)),
                  pl.BlockSpec((d, hw), lambda bi, i, j: (0, j)),
                  pl.BlockSpec((d, LANES), lambda bi, i, j: (0, 0))],
        out_specs=[head_spec] * 3 + [flat_spec] * 4
                  + [pl.BlockSpec((None, tm, LANES), lambda bi, i, j: (bi, i, 0))],
        out_shape=[head_shape] * 3 + [flat_shape] * 4
                  + [jax.ShapeDtypeStruct((b, s, LANES), F32)],
        scratch_shapes=[pltpu.VMEM((tm, d), BF16)],
        compiler_params=_cparams(("parallel", "parallel", "arbitrary")),
        name="inproj",
    )(x, mod, norm_w.reshape(1, d), w_main, w_gate)


def _t5_bucket(rel):
    half = N_BUCKETS // 2
    max_exact = half // 2
    n = np.abs(rel)
    large = max_exact + (np.log(np.maximum(n, 1) / max_exact) / np.log(MAX_DISTANCE / max_exact)
                         * (half - max_exact)).astype(np.int32)
    large = np.minimum(large, half - 1)
    return (np.where(rel > 0, half, 0) + np.where(n < max_exact, n, large)).astype(np.int32)


def _bucket_tables():
    rel16 = np.arange(3 * SIDE)[None, :] - SIDE - np.arange(SIDE)[:, None]
    perm4 = np.array([4 * (i % 16) + i // 16 for i in range(SIDE)])
    col = np.arange(3 * SIDE)
    rel4 = (SIDE * (col // SIDE - 1) + perm4[col % SIDE])[None, :] - perm4[:, None]
    perm1 = np.array([16 * (i % 16) + i // 16 for i in range(256)])
    col = np.arange(768)
    rel1 = (256 * (col // 256 - 1) + perm1[col % 256])[None, :] - perm1[:, None]
    return [np.where(np.abs(rel) <= SIDE, _t5_bucket(rel * dil), N_BUCKETS).astype(np.int32)
            for rel, dil in ((rel1, 1), (rel4, 4), (rel16, 16))]


def _bias_kernel(rb_ref, i1_ref, i4_ref, i16_ref, b1_ref, b4_ref, b16_ref):
    h = pl.program_id(0)
    for idx_ref, out_ref in ((i1_ref, b1_ref), (i4_ref, b4_ref), (i16_ref, b16_ref)):
        idx = idx_ref[...]
        acc = jnp.full(idx.shape, NEG_INF, F32)
        for bucket in range(N_BUCKETS):
            acc = jnp.where(idx == bucket, rb_ref[bucket, h], acc)
        out_ref[...] = acc


def _attn_bias_tables(rel_bias):
    idx = [jnp.asarray(t) for t in _bucket_tables()]
    whole = lambda a: pl.BlockSpec(a.shape, lambda h: (0, 0))
    return pl.pallas_call(
        _bias_kernel,
        grid=(N_HEADS,),
        in_specs=[pl.BlockSpec(memory_space=pltpu.SMEM)] + [whole(a) for a in idx],
        out_specs=[pl.BlockSpec((None,) + a.shape, lambda h: (h, 0, 0)) for a in idx],
        out_shape=[jax.ShapeDtypeStruct((N_HEADS,) + a.shape, F32) for a in idx],
        compiler_params=_cparams(("parallel",)),
        name="attn_bias_tables",
    )(rel_bias.astype(F32), *idx)


def _attn_kernel(q_ref, kp_ref, kc_ref, kn_ref, vp_ref, vc_ref, vn_ref,
                 b1_ref, b4_ref, b16_ref, o_ref,
                 o1_scr, l1_scr, o4_scr, l4_scr, o16_scr, l16_scr):
    t = pl.program_id(2)
    neg_prev = jnp.where(t > 0, 0.0, NEG_INF).astype(F32)
    neg_next = jnp.where(t < pl.num_programs(2) - 1, 0.0, NEG_INF).astype(F32)
    scale = HEAD_DIM ** -0.5

    def lanes(r):
        return slice(r * HEAD_DIM, (r + 1) * HEAD_DIM)

    def edge_row(nk, prev, nxt):
        col = lax.broadcasted_iota(jnp.int32, (1, nk), 1)
        edge = jnp.zeros((1, nk), F32)
        if prev:
            edge = jnp.where(col < nk // 3, neg_prev, edge)
        if nxt:
            edge = jnp.where(col >= 2 * (nk // 3), neg_next, edge)
        return edge

    def attend(q, keys, vals, bias, edge):
        s = jnp.einsum("gmd,gnd->gmn", q, keys, preferred_element_type=F32)
        s = s * scale + bias[None] + edge
        m = jnp.max(s, axis=-1, keepdims=True)
        p = jnp.exp(s - m)
        den = jnp.sum(p, axis=-1, keepdims=True)
        o = jnp.einsum("gmn,gnd->gmd", p.astype(BF16), vals, preferred_element_type=F32) / den
        lse = m + jnp.log(den)
        return o, jnp.broadcast_to(lse, o.shape)

    def nbr16(pref, cref, nref):
        return jnp.stack([jnp.concatenate([pref[:, lanes(r)], cref[:, lanes(r)], nref[:, lanes(r)]], axis=0)
                          for r in range(16)])

    o, l = attend(jnp.stack([q_ref[:, lanes(r)] for r in range(16)]),
                  nbr16(kp_ref, kc_ref, kn_ref), nbr16(vp_ref, vc_ref, vn_ref),
                  b16_ref[...], edge_row(3 * SIDE, True, True)[None])
    for r in range(16):
        o16_scr[:, lanes(r)] = o[r]
        l16_scr[:, lanes(r)] = l[r]

    def blk4(ref, r4, m):
        return jnp.concatenate([ref[16 * m:16 * m + 16, lanes(r4 + 4 * c)] for c in range(4)], axis=0)

    def nbr4(pref, cref, nref):
        out = []
        for r4 in range(4):
            for m in range(4):
                lo = blk4(pref, r4, 3) if m == 0 else blk4(cref, r4, m - 1)
                hi = blk4(nref, r4, 0) if m == 3 else blk4(cref, r4, m + 1)
                out.append(jnp.concatenate([lo, blk4(cref, r4, m), hi], axis=0))
        return jnp.stack(out)

    edge4 = jnp.stack([edge_row(3 * SIDE, m == 0, m == 3) for _ in range(4) for m in range(4)])
    o, l = attend(jnp.stack([blk4(q_ref, r4, m) for r4 in range(4) for m in range(4)]),
                  nbr4(kp_ref, kc_ref, kn_ref), nbr4(vp_ref, vc_ref, vn_ref), b4_ref[...], edge4)
    for r4 in range(4):
        for m in range(4):
            for c in range(4):
                o4_scr[16 * m:16 * m + 16, lanes(r4 + 4 * c)] = o[4 * r4 + m, 16 * c:16 * c + 16]
                l4_scr[16 * m:16 * m + 16, lanes(r4 + 4 * c)] = l[4 * r4 + m, 16 * c:16 * c + 16]

    def grp1(ref, g):
        return jnp.concatenate([ref[16 * g:16 * g + 16, lanes(r)] for r in range(16)], axis=0)

    def nbr1(pref, cref, nref):
        out = []
        for g in range(4):
            lo = grp1(pref, 3) if g == 0 else grp1(cref, g - 1)
            hi = grp1(nref, 0) if g == 3 else grp1(cref, g + 1)
            out.append(jnp.concatenate([lo, grp1(cref, g), hi], axis=0))
        return jnp.stack(out)

    edge1 = jnp.stack([edge_row(768, g == 0, g == 3) for g in range(4)])
    o, l = attend(jnp.stack([grp1(q_ref, g) for g in range(4)]),
                  nbr1(kp_ref, kc_ref, kn_ref), nbr1(vp_ref, vc_ref, vn_ref), b1_ref[...], edge1)
    for g in range(4):
        for r in range(16):
            o1_scr[16 * g:16 * g + 16, lanes(r)] = o[g, 16 * r:16 * r + 16]
            l1_scr[16 * g:16 * g + 16, lanes(r)] = l[g, 16 * r:16 * r + 16]

    l1, l4, l16 = l1_scr[...], l4_scr[...], l16_scr[...]
    mx = jnp.maximum(jnp.maximum(l1, l4), l16)
    w1, w4, w16 = jnp.exp(l1 - mx), jnp.exp(l4 - mx), jnp.exp(l16 - mx)
    num = w1 * o1_scr[...] + w4 * o4_scr[...] + w16 * o16_scr[...]
    o_ref[...] = num / (w1 + w4 + w16)


def _attention(q, k, v, tables):
    b, nh, s, dh = q.shape
    nt = s // ATT_TILE
    rows, width = ATT_TILE // 16, 16 * dh
    view = lambda a: a.reshape(b, nh, nt, rows, width)
    cur = pl.BlockSpec((None, None, None, rows, width), lambda bi, h, t: (bi, h, t, 0, 0))
    prev = pl.BlockSpec((None, None, None, rows, width),
                        lambda bi, h, t: (bi, h, jnp.maximum(t - 1, 0), 0, 0))
    nxt = pl.BlockSpec((None, None, None, rows, width),
                       lambda bi, h, t: (bi, h, jnp.minimum(t + 1, nt - 1), 0, 0))
    b1, b4, b16 = tables
    tspec = lambda a: pl.BlockSpec((None,) + a.shape[1:], lambda bi, h, t: (h, 0, 0))
    out = pl.pallas_call(
        _attn_kernel,
        grid=(b, nh, nt),
        in_specs=[cur, prev, cur, nxt, prev, cur, nxt, tspec(b1), tspec(b4), tspec(b16)],
        out_specs=cur,
        out_shape=jax.ShapeDtypeStruct((b, nh, nt, rows, width), F32),
        scratch_shapes=[pltpu.VMEM((rows, width), F32)] * 6,
        compiler_params=_cparams(("parallel", "parallel", "parallel")),
        name="dilated_attention",
    )(view(q), view(k), view(k), view(k), view(v), view(v), view(v), b1, b4, b16)
    return out.reshape(b, nh, s, dh)


def _dn_prep_kernel(qp_ref, qc_ref, qn_ref, kp_ref, kc_ref, kn_ref, vp_ref, vc_ref, vn_ref,
                    gt_ref, cw_ref, alog_ref, dtb_ref, lvl_ref,
                    uf_ref, wf_ref, qf_ref, kf_ref, af_ref,
                    ub_ref, wb_ref, qb_ref, kb_ref, ab_ref, gl_ref,
                    t_scr, tb_scr, lb_scr, x_scr, rhs_scr):
    i = pl.program_id(1)
    n = DN_TILE
    has_prev = (i > 0).astype(F32)
    has_next = (i < pl.num_programs(1) - 1).astype(F32)
    pad = CONV_K // 2

    def conv_silu(p_ref, c_ref, n_ref, part):
        xt = jnp.concatenate([p_ref[8 - pad:8, :] * has_prev, c_ref[...], n_ref[0:pad, :] * has_next], axis=0)
        acc = jnp.zeros((n, HALF_WIDTH), F32)
        for tap in range(CONV_K):
            w = cw_ref[tap:tap + 1, part * HALF_WIDTH:(part + 1) * HALF_WIDTH]
            acc = acc + xt[tap:tap + n, :] * w
        return _silu(acc)

    q_all = conv_silu(qp_ref, qc_ref, qn_ref, 0)
    k_all = conv_silu(kp_ref, kc_ref, kn_ref, 1)
    v_all = conv_silu(vp_ref, vc_ref, vn_ref, 2)

    gates = gt_ref[...]
    beta_all = jax.nn.sigmoid(gates)
    xa = gates + dtb_ref[...]
    softplus = jnp.maximum(xa, 0.0) + jnp.log1p(jnp.exp(-jnp.abs(xa)))
    g_all = -jnp.exp(alog_ref[...]) * softplus

    row = lax.broadcasted_iota(jnp.int32, (n, n), 0)
    col = lax.broadcasted_iota(jnp.int32, (n, n), 1)
    same = (row // CHUNK) == (col // CHUNK)
    hp = lax.Precision.HIGHEST
    tri_f = jnp.where(same & (col <= row), 1.0, 0.0).astype(F32)
    tri_b = jnp.where(same & (col >= row), 1.0, 0.0).astype(F32)
    gc_f = jnp.dot(tri_f, g_all, precision=hp, preferred_element_type=F32)
    gc_b = jnp.dot(tri_b, g_all, precision=hp, preferred_element_type=F32)
    gtot = jnp.dot(jnp.where(same, 1.0, 0.0).astype(F32), g_all, precision=hp, preferred_element_type=F32)
    gl_ref[...] = jnp.exp(gtot)
    gc_t = (gc_f.T, gc_b.T)
    gc = (gc_f, gc_b)
    incl = (same & (col <= row), same & (col >= row))
    offdiag = row != col
    eye = jnp.where(offdiag, 0.0, 1.0).astype(F32)
    level0 = ((row // 2) == (col // 2)) & offdiag
    outs = ((uf_ref, wf_ref, qf_ref, kf_ref, af_ref), (ub_ref, wb_ref, qb_ref, kb_ref, ab_ref))
    qscale = HEAD_DIM ** -0.5

    n_chain = 2 * N_HEADS
    nt_dims = (((1,), (1,)), ((), ()))

    for hd in range(N_HEADS):
        hl = slice(hd * HEAD_DIM, (hd + 1) * HEAD_DIM)
        q = q_all[:, hl]
        k = k_all[:, hl]
        v = v_all[:, hl]
        q = q * lax.rsqrt(jnp.sum(q * q, axis=-1, keepdims=True) + EPS) * qscale
        k = k * lax.rsqrt(jnp.sum(k * k, axis=-1, keepdims=True) + EPS)
        kb16 = k.astype(BF16)
        kk = lax.dot_general(kb16, kb16, nt_dims, preferred_element_type=F32)
        qk = lax.dot_general(q.astype(BF16), kb16, nt_dims, preferred_element_type=F32)
        for d in range(2):
            c = 2 * hd + d
            cb = d * N_HEADS + hd
            cg = 2 * N_HEADS + cb
            beta = beta_all[:, cb:cb + 1]
            gcc = gc[d][:, cg:cg + 1]
            gcr = gc_t[d][cg:cg + 1, :]
            decay = jnp.exp(jnp.where(incl[d], gcc - gcr, NEG_INF))
            lmat = jnp.where(offdiag, beta * kk * decay, 0.0)
            t2 = eye - jnp.where(level0, lmat, 0.0)
            t_scr[c] = t2
            tb_scr[c] = t2.astype(BF16)
            lb_scr[c] = lmat.astype(BF16)
            egc = jnp.exp(gcc)
            rhs_scr[c] = jnp.concatenate([v * beta, k * (beta * egc)], axis=1).astype(BF16)
            _, _, qd_ref, kd_ref, a_ref = outs[d]
            qd_ref[:, hl] = (q * egc).astype(BF16)
            kd_ref[:, hl] = (k * jnp.exp(gtot[:, cg:cg + 1] - gcc)).astype(BF16)
            aqk = (qk * decay).astype(BF16)
            for cc in range(n // CHUNK):
                a_ref[cc * CHUNK:(cc + 1) * CHUNK, hd * CHUNK:(hd + 1) * CHUNK] = \
                    aqk[cc * CHUNK:(cc + 1) * CHUNK, cc * CHUNK:(cc + 1) * CHUNK]

    n_lv = lvl_ref.shape[0]
    for lv in range(n_lv):
        for c in range(n_chain):
            x_scr[c] = jnp.dot(lb_scr[c] * lvl_ref[lv], tb_scr[c], preferred_element_type=F32).astype(BF16)
        for c in range(n_chain):
            t = t_scr[c] - jnp.dot(tb_scr[c], x_scr[c], preferred_element_type=F32)
            if lv + 1 < n_lv:
                t_scr[c] = t
            tb_scr[c] = t.astype(BF16)

    for hd in range(N_HEADS):
        hl = slice(hd * HEAD_DIM, (hd + 1) * HEAD_DIM)
        for d in range(2):
            c = 2 * hd + d
            sol = jnp.dot(tb_scr[c], rhs_scr[c], preferred_element_type=F32)
            outs[d][0][:, hl] = sol[:, :HEAD_DIM]
            outs[d][1][:, hl] = sol[:, HEAD_DIM:].astype(BF16)


def _dn_prep(dq, dk, dv, gates, conv_w, alog_row, dtb_row):
    b, s, hw = dq.shape
    n = DN_TILE
    nb = s // n
    r8 = n // 8
    cur = pl.BlockSpec((None, n, hw), lambda bi, i: (bi, i, 0))
    prev = pl.BlockSpec((None, 8, hw), lambda bi, i: (bi, jnp.maximum(i * r8 - 1, 0), 0))
    nxt = pl.BlockSpec((None, 8, hw), lambda bi, i: (bi, jnp.minimum((i + 1) * r8, s // 8 - 1), 0))
    small = pl.BlockSpec((None, n, LANES), lambda bi, i: (bi, i, 0))
    aspec = pl.BlockSpec((None, n, N_HEADS * CHUNK), lambda bi, i: (bi, i, 0))
    const = lambda a: pl.BlockSpec(a.shape, lambda bi, i: (0, 0))
    f32s = jax.ShapeDtypeStruct((b, s, hw), F32)
    b16s = jax.ShapeDtypeStruct((b, s, hw), BF16)
    a16s = jax.ShapeDtypeStruct((b, s, N_HEADS * CHUNK), BF16)
    per_dir_specs = [cur, cur, cur, cur, aspec]
    per_dir_shapes = [f32s, b16s, b16s, b16s, a16s]
    idx = np.arange(n)
    lvl = np.stack([((idx[:, None] // (2 * sz)) == (idx[None, :] // (2 * sz)))
                    & ((idx[:, None] // sz) != (idx[None, :] // sz)) for sz in (2, 4, 8, 16, 32)])
    lvl = jnp.asarray(lvl.astype(np.float32)).astype(BF16)
    n_chain = 2 * N_HEADS
    return pl.pallas_call(
        _dn_prep_kernel,
        grid=(b, nb),
        in_specs=[prev, cur, nxt] * 3 + [small, const(conv_w), const(alog_row), const(dtb_row),
                                         pl.BlockSpec(lvl.shape, lambda bi, i: (0, 0, 0))],
        out_specs=per_dir_specs * 2 + [small],
        out_shape=per_dir_shapes * 2 + [jax.ShapeDtypeStruct((b, s, LANES), F32)],
        scratch_shapes=[pltpu.VMEM((n_chain, n, n), F32), pltpu.VMEM((n_chain, n, n), BF16),
                        pltpu.VMEM((n_chain, n, n), BF16), pltpu.VMEM((n_chain, n, n), BF16),
                        pltpu.VMEM((n_chain, n, 2 * HEAD_DIM), BF16)],
        compiler_params=_cparams(("parallel", "parallel")),
        name="deltanet_prep",
    )(dq, dq, dq, dk, dk, dk, dv, dv, dv, gates, conv_w, alog_row, dtb_row, lvl)


def _dn_scan_kernel(uf_ref, wf_ref, qf_ref, kf_ref, af_ref, glf_ref,
                    ub_ref, wb_ref, qb_ref, kb_ref, ab_ref, glb_ref,
                    of_ref, ob_ref, state_scr):
    @pl.when(pl.program_id(1) == 0)
    def _():
        state_scr[...] = jnp.zeros_like(state_scr)

    ins = ((uf_ref, wf_ref, qf_ref, kf_ref, af_ref, glf_ref, of_ref),
           (ub_ref, wb_ref, qb_ref, kb_ref, ab_ref, glb_ref, ob_ref))
    nchunks = SCAN_TILE // CHUNK

    chains = [(d, hd) for d in range(2) for hd in range(N_HEADS)]

    def body(ci, carry):
        rows, gl_rows = [], []
        for d in range(2):
            cidx = ci if d == 0 else nchunks - 1 - ci
            r0 = pl.multiple_of(cidx * CHUNK, CHUNK)
            rows.append(pl.ds(r0, CHUNK))
            gl_rows.append(ins[d][5][pl.ds(r0, 1), :])
        hls = [slice(hd * HEAD_DIM, (hd + 1) * HEAD_DIM) for hd in range(N_HEADS)]
        ws = []
        for d, hd in chains:
            _, w_ref, qd_ref = ins[d][:3]
            wq = jnp.concatenate([w_ref[rows[d], hls[hd]], qd_ref[rows[d], hls[hd]]], axis=0)
            ws.append(jnp.dot(wq, state_scr[d, hd].astype(BF16), preferred_element_type=F32))
        vbs = [(ins[d][0][rows[d], hls[hd]] - ws[i][:CHUNK]).astype(BF16) for i, (d, hd) in enumerate(chains)]
        for i, (d, hd) in enumerate(chains):
            kd_ref, a_ref, o_ref = ins[d][3], ins[d][4], ins[d][6]
            a = a_ref[rows[d], hd * CHUNK:(hd + 1) * CHUNK]
            o_ref[rows[d], hls[hd]] = ws[i][CHUNK:] + jnp.dot(a, vbs[i], preferred_element_type=F32)
            upd = lax.dot_general(kd_ref[rows[d], hls[hd]], vbs[i], (((0,), (0,)), ((), ())),
                                  preferred_element_type=F32)
            cg = 2 * N_HEADS + d * N_HEADS + hd
            state_scr[d, hd] = state_scr[d, hd] * gl_rows[d][:, cg:cg + 1] + upd
        return carry

    lax.fori_loop(0, nchunks, body, 0)


def _dn_scan(prep):
    uf, wf, qf, kf, af, ub, wb, qb, kb, ab, gl = prep
    b, s, hw = uf.shape
    n = SCAN_TILE
    nb = s // n
    fwd = lambda w: pl.BlockSpec((None, n, w), lambda bi, i: (bi, i, 0))
    bwd = lambda w: pl.BlockSpec((None, n, w), lambda bi, i: (bi, nb - 1 - i, 0))
    widths = [hw, hw, hw, hw, N_HEADS * CHUNK, LANES]
    return pl.pallas_call(
        _dn_scan_kernel,
        grid=(b, nb),
        in_specs=[fwd(w) for w in widths] + [bwd(w) for w in widths],
        out_specs=[fwd(hw), bwd(hw)],
        out_shape=[jax.ShapeDtypeStruct((b, s, hw), F32)] * 2,
        scratch_shapes=[pltpu.VMEM((2, N_HEADS, HEAD_DIM, HEAD_DIM), F32)],
        compiler_params=_cparams(("parallel", "arbitrary")),
        name="deltanet_scan",
    )(uf, wf, qf, kf, af, gl, ub, wb, qb, kb, ab, gl)


def _outproj_kernel(x_ref, mod_ref, oa_ref, of_ref, ob_ref, z_ref, na_ref, nd_ref, w_ref, o_ref):
    oa = jnp.concatenate([oa_ref[hd] for hd in range(N_HEADS)], axis=-1)
    oa = _rms(oa, na_ref[...])
    od = of_ref[...] + ob_ref[...]
    z = z_ref[...]
    parts = []
    for hd in range(N_HEADS):
        hl = slice(hd * HEAD_DIM, (hd + 1) * HEAD_DIM)
        parts.append(_rms(od[:, hl], nd_ref[...]) * _silu(z[:, hl]))
    y = jnp.concatenate([oa] + parts, axis=-1).astype(BF16)
    out = jnp.dot(y, w_ref[...], preferred_element_type=F32)
    o_ref[...] = x_ref[...] + mod_ref[5:6, :] * out


def _outproj(x, mod, o_attn, o_f, o_b, z, norm_attn, norm_dn, w_out):
    b, s, d = x.shape
    tm = ROW_TILE
    hw = HALF_WIDTH
    flat = pl.BlockSpec((None, tm, hw), lambda bi, i: (bi, i, 0))
    return pl.pallas_call(
        _outproj_kernel,
        grid=(b, s // tm),
        in_specs=[pl.BlockSpec((None, tm, d), lambda bi, i: (bi, i, 0)),
                  pl.BlockSpec((None, N_MOD, d), lambda bi, i: (bi, 0, 0)),
                  pl.BlockSpec((None, N_HEADS, tm, HEAD_DIM), lambda bi, i: (bi, 0, i, 0)),
                  flat, flat, flat,
                  pl.BlockSpec((1, hw), lambda bi, i: (0, 0)),
                  pl.BlockSpec((1, HEAD_DIM), lambda bi, i: (0, 0)),
                  pl.BlockSpec((d, d), lambda bi, i: (0, 0))],
        out_specs=pl.BlockSpec((None, tm, d), lambda bi, i: (bi, i, 0)),
        out_shape=jax.ShapeDtypeStruct((b, s, d), F32),
        compiler_params=_cparams(("parallel", "parallel")),
        name="outproj",
    )(x, mod, o_attn, o_f, o_b, z, norm_attn.reshape(1, hw), norm_dn.reshape(1, HEAD_DIM), w_out)


def _pad_lanes(vec, offset):
    flat = vec.reshape(-1).astype(F32)
    return jnp.zeros((1, LANES), F32).at[0, offset:offset + flat.shape[0]].set(flat)


def _trunk(x, mod, p):
    x = _ffn(x, mod, p["norm_ffn1"], p["w_ffn1_in"], p["w_ffn1_out"], p["norm_final"], sub=0, final=False)
    aq, ak, av, dq, dk, dv, z, gates = _inproj(x, mod, p["norm_mix"], p["w_in_main"], p["w_in_gate"])
    o_attn = _attention(aq, ak, av, p["bias_tables"])
    prep = _dn_prep(dq, dk, dv, gates, p["conv_w"], p["alog_row"], p["dtb_row"])
    o_f, o_b = _dn_scan(prep)
    x = _outproj(x, mod, o_attn, o_f, o_b, z, p["norm_attn_out"], p["norm_dn_out"], p["w_out"])
    return _ffn(x, mod, p["norm_ffn2"], p["w_ffn2_in"], p["w_ffn2_out"], p["norm_final"], sub=2, final=True)


def kernel(x_prompt, x_sample, c_prompt, c_sample, w_mod, b_mod, norm_ffn1, w_ffn1_in, w_ffn1_out, norm_mix, w_in, conv_w, a_log, dt_bias, norm_attn_out, norm_dn_out, w_out, norm_ffn2, w_ffn2_in, w_ffn2_out, rel_bias, norm_final):
    assert w_mod.shape[0] == 1, "one encoder layer"
    bp, bs = c_prompt.shape[0], c_sample.shape[0]
    n_gate = 4 * N_HEADS
    cut = 3 * HALF_WIDTH + 4 * HALF_WIDTH
    p = {
        "norm_ffn1": norm_ffn1[0], "w_ffn1_in": w_ffn1_in[0].astype(BF16), "w_ffn1_out": w_ffn1_out[0].astype(BF16),
        "norm_mix": norm_mix[0],
        "w_in_main": w_in[0, :, :cut].astype(BF16),
        "w_in_gate": jnp.pad(w_in[0, :, cut:], ((0, 0), (0, LANES - n_gate))).astype(BF16),
        "conv_w": conv_w[0],
        "alog_row": _pad_lanes(a_log[0], 2 * N_HEADS), "dtb_row": _pad_lanes(dt_bias[0], 2 * N_HEADS),
        "norm_attn_out": norm_attn_out[0], "norm_dn_out": norm_dn_out[0], "w_out": w_out[0].astype(BF16),
        "norm_ffn2": norm_ffn2[0], "w_ffn2_in": w_ffn2_in[0].astype(BF16), "w_ffn2_out": w_ffn2_out[0].astype(BF16),
        "norm_final": norm_final,
        "bias_tables": _attn_bias_tables(rel_bias),
    }
    c_all = jnp.concatenate([c_prompt, c_sample, jnp.zeros((8 - (bp + bs) % 8, D_MODEL), F32)], axis=0)
    mod = _modulation(c_all, w_mod[0], b_mod[0]).reshape(c_all.shape[0], N_MOD, D_MODEL)
    y_prompt = _trunk(x_prompt, mod[:bp], p)
    y_sample = _trunk(x_sample, mod[bp:bp + bs], p)
    return (y_prompt, y_sample)
```

```python
import functools

import numpy as np
import jax
import jax.numpy as jnp
from jax import lax
from jax.experimental import pallas as pl
from jax.experimental.pallas import tpu as pltpu

F32 = jnp.float32
BF16 = jnp.bfloat16

D_MODEL = 2048
N_HEADS = 8
HEAD_DIM = 128
HALF_WIDTH = N_HEADS * HEAD_DIM
N_MOD = 9
N_BUCKETS = 32
MAX_DISTANCE = 1024
CONV_K = 5
CHUNK = 64
SIDE = 64
ATT_TILE = 1024
EPS = 1e-6
NEG_INF = -1e30
LOG2E = 1.4426950408889634
Q_SCALE_LOG2 = HEAD_DIM ** -0.5 * LOG2E
FFN_RESIDUAL = 0.5

LANES = 128
VMEM_LIMIT = 56 * 1024 * 1024

ROW_TILE = 512
FF_TILE = 512
MOD_TILE = 1024
DN_TILE = 256
HALO = 16
SCAN_TILE = 512


def _cparams(sem):
    return pltpu.CompilerParams(dimension_semantics=sem, vmem_limit_bytes=VMEM_LIMIT)


def _rms(x, w):
    return x * lax.rsqrt(jnp.mean(x * x, axis=-1, keepdims=True) + EPS) * w


def _silu(x):
    return x * jax.nn.sigmoid(x)


def _mod_kernel(c_ref, w_ref, b_ref, o_ref):
    a = _silu(c_ref[...])
    o_ref[...] = jnp.dot(a, w_ref[...], precision=lax.Precision.HIGHEST,
                         preferred_element_type=F32) + b_ref[...]


def _modulation(c, w_mod, b_mod):
    rows = c.shape[0]
    n = w_mod.shape[1]
    return pl.pallas_call(
        _mod_kernel,
        grid=(n // MOD_TILE,),
        in_specs=[pl.BlockSpec((rows, D_MODEL), lambda j: (0, 0)),
                  pl.BlockSpec((D_MODEL, MOD_TILE), lambda j: (0, j)),
                  pl.BlockSpec((1, MOD_TILE), lambda j: (0, j))],
        out_specs=pl.BlockSpec((rows, MOD_TILE), lambda j: (0, j)),
        out_shape=jax.ShapeDtypeStruct((rows, n), F32),
        compiler_params=_cparams(("parallel",)),
        name="modulation",
    )(c, w_mod, b_mod.reshape(1, n))


def _ffn_kernel(x_ref, mod_ref, nw_ref, wg_ref, wu_ref, wo_ref, fw_ref, o_ref,
                h_scr, acc_scr, *, sub, final):
    j = pl.program_id(2)

    @pl.when(j == 0)
    def _():
        shift = mod_ref[3 * sub:3 * sub + 1, :]
        scale = mod_ref[3 * sub + 1:3 * sub + 2, :]
        h = _rms(x_ref[...], nw_ref[...] * (1.0 + scale)) + shift
        h_scr[...] = h.astype(BF16)
        acc_scr[...] = jnp.zeros_like(acc_scr)

    h = h_scr[...]
    g = jnp.dot(h, wg_ref[...], preferred_element_type=F32)
    u = jnp.dot(h, wu_ref[...], preferred_element_type=F32)
    a = (_silu(g) * u).astype(BF16)
    acc_scr[...] += jnp.dot(a, wo_ref[...], preferred_element_type=F32)

    @pl.when(j == pl.num_programs(2) - 1)
    def _():
        gate = mod_ref[3 * sub + 2:3 * sub + 3, :]
        out = x_ref[...] + FFN_RESIDUAL * gate * acc_scr[...]
        if final:
            out = _rms(out, fw_ref[...])
        o_ref[...] = out


def _ffn(x, mod, norm_w, w_in, w_out, final_w, *, sub, final):
    b, s, d = x.shape
    d_ff = w_out.shape[0]
    nf = d_ff // FF_TILE
    tm = ROW_TILE
    kern = functools.partial(_ffn_kernel, sub=sub, final=final)
    return pl.pallas_call(
        kern,
        grid=(b, s // tm, nf),
        in_specs=[pl.BlockSpec((None, tm, d), lambda bi, i, j: (bi, i, 0)),
                  pl.BlockSpec((None, N_MOD, d), lambda bi, i, j: (bi, 0, 0)),
                  pl.BlockSpec((1, d), lambda bi, i, j: (0, 0)),
                  pl.BlockSpec((d, FF_TILE), lambda bi, i, j: (0, j)),
                  pl.BlockSpec((d, FF_TILE), lambda bi, i, j: (0, j + nf)),
                  pl.BlockSpec((FF_TILE, d), lambda bi, i, j: (j, 0)),
                  pl.BlockSpec((1, d), lambda bi, i, j: (0, 0))],
        out_specs=pl.BlockSpec((None, tm, d), lambda bi, i, j: (bi, i, 0)),
        out_shape=jax.ShapeDtypeStruct((b, s, d), F32),
        scratch_shapes=[pltpu.VMEM((tm, d), BF16), pltpu.VMEM((tm, d), F32)],
        compiler_params=_cparams(("parallel", "parallel", "arbitrary")),
        name="ffn%d" % sub,
    )(x, mod, norm_w.reshape(1, d), w_in, w_in, w_out, final_w.reshape(1, d))


def _inproj_kernel(x_ref, mod_ref, nw_ref, w_ref, wg_ref,
                   aq_ref, ak_ref, av_ref, dq_ref, dk_ref, dv_ref, z_ref, gt_ref, h_scr):
    j = pl.program_id(2)

    @pl.when(j == 0)
    def _():
        shift = mod_ref[3:4, :]
        scale = mod_ref[4:5, :]
        h = _rms(x_ref[...], nw_ref[...] * (1.0 + scale)) + shift
        h_scr[...] = h.astype(BF16)
        gt_ref[...] = jnp.dot(h_scr[...], wg_ref[...], preferred_element_type=F32)

    y = jnp.dot(h_scr[...], w_ref[...], preferred_element_type=F32)

    def write_heads(ref, mult):
        for hd in range(N_HEADS):
            ref[hd] = (y[:, hd * HEAD_DIM:(hd + 1) * HEAD_DIM] * mult).astype(BF16)

    def write_flat(ref):
        ref[...] = y.astype(BF16)

    for idx, ref in enumerate((aq_ref, ak_ref, av_ref)):
        pl.when(j == idx)(functools.partial(write_heads, ref, Q_SCALE_LOG2 if idx == 0 else 1.0))
    for idx, ref in enumerate((dq_ref, dk_ref, dv_ref, z_ref)):
        pl.when(j == idx + 3)(functools.partial(write_flat, ref))


def _inproj(x, mod, norm_w, w_main, w_gate):
    b, s, d = x.shape
    tm = ROW_TILE
    hw = HALF_WIDTH
    head_spec = pl.BlockSpec((None, N_HEADS, tm, HEAD_DIM), lambda bi, i, j: (bi, 0, i, 0))
    flat_spec = pl.BlockSpec((None, tm, hw), lambda bi, i, j: (bi, i, 0))
    head_shape = jax.ShapeDtypeStruct((b, N_HEADS, s, HEAD_DIM), BF16)
    flat_shape = jax.ShapeDtypeStruct((b, s, hw), BF16)
    return pl.pallas_call(
        _inproj_kernel,
        grid=(b, s // tm, 7),
        in_specs=[pl.BlockSpec((None, tm, d), lambda bi, i, j: (bi, i, 0)),
                  pl.BlockSpec((None, N_MOD, d), lambda bi, i, j: (bi, 0, 0)),
                  pl.BlockSpec((1, d), lambda bi, i, j: (0, 0)),
                  pl.BlockSpec((d, hw), lambda bi, i, j: (0, j)),
                  pl.BlockSpec((d, LANES), lambda bi, i, j: (0, 0))],
        out_specs=[head_spec] * 3 + [flat_spec] * 4
                  + [pl.BlockSpec((None, tm, LANES), lambda bi, i, j: (bi, i, 0))],
        out_shape=[head_shape] * 3 + [flat_shape] * 4
                  + [jax.ShapeDtypeStruct((b, s, LANES), F32)],
        scratch_shapes=[pltpu.VMEM((tm, d), BF16)],
        compiler_params=_cparams(("parallel", "parallel", "arbitrary")),
        name="inproj",
    )(x, mod, norm_w.reshape(1, d), w_main, w_gate)


def _t5_bucket(rel):
    half = N_BUCKETS // 2
    max_exact = half // 2
    n = np.abs(rel)
    large = max_exact + (np.log(np.maximum(n, 1) / max_exact) / np.log(MAX_DISTANCE / max_exact)
                         * (half - max_exact)).astype(np.int32)
    large = np.minimum(large, half - 1)
    return (np.where(rel > 0, half, 0) + np.where(n < max_exact, n, large)).astype(np.int32)


def _bucket_tables():
    rel16 = np.arange(3 * SIDE)[None, :] - SIDE - np.arange(SIDE)[:, None]
    perm4 = np.array([4 * (i % 16) + i // 16 for i in range(SIDE)])
    col = np.arange(3 * SIDE)
    rel4 = (SIDE * (col // SIDE - 1) + perm4[col % SIDE])[None, :] - perm4[:, None]
    perm1 = np.array([16 * (i % 16) + i // 16 for i in range(256)])
    col = np.arange(768)
    rel1 = (256 * (col // 256 - 1) + perm1[col % 256])[None, :] - perm1[:, None]
    return [np.where(np.abs(rel) <= SIDE, _t5_bucket(rel * dil), N_BUCKETS).astype(np.int32)
            for rel, dil in ((rel1, 1), (rel4, 4), (rel16, 16))]


def _bias_kernel(rb_ref, i1_ref, i4_ref, i16_ref, b1_ref, b4_ref, b16_ref):
    h = pl.program_id(0)
    for idx_ref, out_ref in ((i1_ref, b1_ref), (i4_ref, b4_ref), (i16_ref, b16_ref)):
        idx = idx_ref[...]
        acc = jnp.full(idx.shape, NEG_INF, F32)
        for bucket in range(N_BUCKETS):
            acc = jnp.where(idx == bucket, rb_ref[bucket, h] * LOG2E, acc)
        out_ref[...] = acc


def _attn_bias_tables(rel_bias):
    idx = [jnp.asarray(t) for t in _bucket_tables()]
    whole = lambda a: pl.BlockSpec(a.shape, lambda h: (0, 0))
    return pl.pallas_call(
        _bias_kernel,
        grid=(N_HEADS,),
        in_specs=[pl.BlockSpec(memory_space=pltpu.SMEM)] + [whole(a) for a in idx],
        out_specs=[pl.BlockSpec((None,) + a.shape, lambda h: (h, 0, 0)) for a in idx],
        out_shape=[jax.ShapeDtypeStruct((N_HEADS,) + a.shape, F32) for a in idx],
        compiler_params=_cparams(("parallel",)),
        name="attn_bias_tables",
    )(rel_bias.astype(F32), *idx)


def _attn_kernel(q_ref, kp_ref, kc_ref, kn_ref, vp_ref, vc_ref, vn_ref,
                 b1_ref, b4_ref, b16_ref, o_ref,
                 o1_scr, l1_scr, o4_scr, l4_scr, o16_scr, l16_scr):
    t = pl.program_id(2)
    neg_prev = jnp.where(t > 0, 0.0, NEG_INF).astype(F32)
    neg_next = jnp.where(t < pl.num_programs(2) - 1, 0.0, NEG_INF).astype(F32)

    def lanes(r):
        return slice(r * HEAD_DIM, (r + 1) * HEAD_DIM)

    def edge_row(nk, prev, nxt):
        col = lax.broadcasted_iota(jnp.int32, (1, nk), 1)
        edge = jnp.zeros((1, nk), F32)
        if prev:
            edge = jnp.where(col < nk // 3, neg_prev, edge)
        if nxt:
            edge = jnp.where(col >= 2 * (nk // 3), neg_next, edge)
        return edge

    def attend(q, keys, vals, biases):
        s = jnp.einsum("gmd,gnd->gmn", q, keys, preferred_element_type=F32)
        s = jnp.stack([s[g] + biases[g] for g in range(len(biases))])
        m = jnp.max(s, axis=-1, keepdims=True)
        p = jnp.exp2(s - m)
        den = jnp.sum(p, axis=-1, keepdims=True)
        o = jnp.einsum("gmn,gnd->gmd", p.astype(BF16), vals, preferred_element_type=F32) / den
        lse = m + jnp.log2(den)
        return o, jnp.broadcast_to(lse, o.shape)

    def edge_biases(bias, nk, flags):
        variants = {}
        for f in set(flags):
            variants[f] = bias + edge_row(nk, *f) if any(f) else bias
        return [variants[f] for f in flags]

    def nbr16(pref, cref, nref):
        return jnp.stack([jnp.concatenate([pref[:, lanes(r)], cref[:, lanes(r)], nref[:, lanes(r)]], axis=0)
                          for r in range(16)])

    o, l = attend(jnp.stack([q_ref[:, lanes(r)] for r in range(16)]),
                  nbr16(kp_ref, kc_ref, kn_ref), nbr16(vp_ref, vc_ref, vn_ref),
                  edge_biases(b16_ref[...], 3 * SIDE, [(True, True)] * 16))
    for r in range(16):
        o16_scr[:, lanes(r)] = o[r]
        l16_scr[:, lanes(r)] = l[r]

    def blk4(ref, r4, m):
        return jnp.concatenate([ref[16 * m:16 * m + 16, lanes(r4 + 4 * c)] for c in range(4)], axis=0)

    def nbr4(pref, cref, nref):
        out = []
        for r4 in range(4):
            for m in range(4):
                lo = blk4(pref, r4, 3) if m == 0 else blk4(cref, r4, m - 1)
                hi = blk4(nref, r4, 0) if m == 3 else blk4(cref, r4, m + 1)
                out.append(jnp.concatenate([lo, blk4(cref, r4, m), hi], axis=0))
        return jnp.stack(out)

    o, l = attend(jnp.stack([blk4(q_ref, r4, m) for r4 in range(4) for m in range(4)]),
                  nbr4(kp_ref, kc_ref, kn_ref), nbr4(vp_ref, vc_ref, vn_ref),
                  edge_biases(b4_ref[...], 3 * SIDE, [(m == 0, m == 3) for _ in range(4) for m in range(4)]))
    for r4 in range(4):
        for m in range(4):
            for c in range(4):
                o4_scr[16 * m:16 * m + 16, lanes(r4 + 4 * c)] = o[4 * r4 + m, 16 * c:16 * c + 16]
                l4_scr[16 * m:16 * m + 16, lanes(r4 + 4 * c)] = l[4 * r4 + m, 16 * c:16 * c + 16]

    def grp1(ref, g):
        return jnp.concatenate([ref[16 * g:16 * g + 16, lanes(r)] for r in range(16)], axis=0)

    def nbr1(pref, cref, nref):
        out = []
        for g in range(4):
            lo = grp1(pref, 3) if g == 0 else grp1(cref, g - 1)
            hi = grp1(nref, 0) if g == 3 else grp1(cref, g + 1)
            out.append(jnp.concatenate([lo, grp1(cref, g), hi], axis=0))
        return jnp.stack(out)

    o, l = attend(jnp.stack([grp1(q_ref, g) for g in range(4)]),
                  nbr1(kp_ref, kc_ref, kn_ref), nbr1(vp_ref, vc_ref, vn_ref),
                  edge_biases(b1_ref[...], 768, [(g == 0, g == 3) for g in range(4)]))
    for g in range(4):
        for r in range(16):
            o1_scr[16 * g:16 * g + 16, lanes(r)] = o[g, 16 * r:16 * r + 16]
            l1_scr[16 * g:16 * g + 16, lanes(r)] = l[g, 16 * r:16 * r + 16]

    l1, l4, l16 = l1_scr[...], l4_scr[...], l16_scr[...]
    mx = jnp.maximum(jnp.maximum(l1, l4), l16)
    w1, w4, w16 = jnp.exp2(l1 - mx), jnp.exp2(l4 - mx), jnp.exp2(l16 - mx)
    num = w1 * o1_scr[...] + w4 * o4_scr[...] + w16 * o16_scr[...]
    o_ref[...] = (num / (w1 + w4 + w16)).astype(o_ref.dtype)


def _attention(q, k, v, tables):
    b, nh, s, dh = q.shape
    nt = s // ATT_TILE
    rows, width = ATT_TILE // 16, 16 * dh
    view = lambda a: a.reshape(b, nh, nt, rows, width)
    cur = pl.BlockSpec((None, None, None, rows, width), lambda bi, h, t: (bi, h, t, 0, 0))
    prev = pl.BlockSpec((None, None, None, rows, width),
                        lambda bi, h, t: (bi, h, jnp.maximum(t - 1, 0), 0, 0))
    nxt = pl.BlockSpec((None, None, None, rows, width),
                       lambda bi, h, t: (bi, h, jnp.minimum(t + 1, nt - 1), 0, 0))
    b1, b4, b16 = tables
    tspec = lambda a: pl.BlockSpec((None,) + a.shape[1:], lambda bi, h, t: (h, 0, 0))
    out = pl.pallas_call(
        _attn_kernel,
        grid=(b, nh, nt),
        in_specs=[cur, prev, cur, nxt, prev, cur, nxt, tspec(b1), tspec(b4), tspec(b16)],
        out_specs=cur,
        out_shape=jax.ShapeDtypeStruct((b, nh, nt, rows, width), BF16),
        scratch_shapes=[pltpu.VMEM((rows, width), F32)] * 6,
        compiler_params=_cparams(("parallel", "parallel", "parallel")),
        name="dilated_attention",
    )(view(q), view(k), view(k), view(k), view(v), view(v), view(v), b1, b4, b16)
    return out.reshape(b, nh, s, dh)


def _dn_prep_kernel(qp_ref, qc_ref, qn_ref, kp_ref, kc_ref, kn_ref, vp_ref, vc_ref, vn_ref,
                    gt_ref, cw_ref, alog_ref, dtb_ref, lvl_ref,
                    uf_ref, wf_ref, qf_ref, kf_ref, af_ref,
                    ub_ref, wb_ref, qb_ref, kb_ref, ab_ref, gl_ref,
                    tb_scr, lb_scr, x_scr, rhs_scr):
    i = pl.program_id(1)
    n = DN_TILE
    has_prev = (i > 0).astype(F32)
    has_next = (i < pl.num_programs(1) - 1).astype(F32)
    pad = CONV_K // 2

    sub8 = lax.broadcasted_iota(jnp.int32, (8, HEAD_DIM), 0)

    def conv_silu(p_ref, c_ref, n_ref, part, hl):
        xc = c_ref[:, hl].astype(F32)
        pv = p_ref[HALO - 8:HALO, hl].astype(F32) * has_prev
        nx = n_ref[0:8, hl].astype(F32) * has_next
        lane0 = part * HALF_WIDTH + hl.start
        acc = xc * cw_ref[pad:pad + 1, lane0:lane0 + HEAD_DIM]
        for tap in range(CONV_K):
            sh = pad - tap
            if sh == 0:
                continue
            if sh > 0:
                rolled = pltpu.roll(xc, sh, 0)
                edge = jnp.where(sub8 < sh, pltpu.roll(pv, sh, 0), rolled[0:8])
                shifted = jnp.concatenate([edge, rolled[8:]], axis=0)
            else:
                rolled = pltpu.roll(xc, n + sh, 0)
                edge = jnp.where(sub8 >= 8 + sh, pltpu.roll(nx, 8 + sh, 0), rolled[n - 8:])
                shifted = jnp.concatenate([rolled[:n - 8], edge], axis=0)
            acc = acc + shifted * cw_ref[tap:tap + 1, lane0:lane0 + HEAD_DIM]
        return _silu(acc)

    gates = gt_ref[...]
    beta_all = jax.nn.sigmoid(gates)
    xa = gates + dtb_ref[...]
    softplus = jnp.maximum(xa, 0.0) + jnp.log1p(jnp.exp(-jnp.abs(xa)))
    g_all = -jnp.exp(alog_ref[...]) * softplus

    row = lax.broadcasted_iota(jnp.int32, (n, n), 0)
    col = lax.broadcasted_iota(jnp.int32, (n, n), 1)
    same = (row // CHUNK) == (col // CHUNK)
    hp = lax.Precision.HIGHEST
    tri_f = jnp.where(same & (col <= row), 1.0, 0.0).astype(F32)
    tri_b = jnp.where(same & (col >= row), 1.0, 0.0).astype(F32)
    gc_f = jnp.dot(tri_f, g_all, precision=hp, preferred_element_type=F32)
    gc_b = jnp.dot(tri_b, g_all, precision=hp, preferred_element_type=F32)
    gtot = jnp.dot(jnp.where(same, 1.0, 0.0).astype(F32), g_all, precision=hp, preferred_element_type=F32)
    gl_ref[...] = jnp.exp(gtot)
    gc_t = (gc_f.T, gc_b.T)
    gc = (gc_f, gc_b)
    incl = (same & (col <= row), same & (col >= row))
    offdiag = row != col
    eye = jnp.where(offdiag, 0.0, 1.0).astype(F32)
    level0 = ((row // 2) == (col // 2)) & offdiag
    outs = ((uf_ref, wf_ref, qf_ref, kf_ref, af_ref), (ub_ref, wb_ref, qb_ref, kb_ref, ab_ref))
    qscale = HEAD_DIM ** -0.5

    nt_dims = (((1,), (1,)), ((), ()))

    def phase_a(hd):
        hl = slice(hd * HEAD_DIM, (hd + 1) * HEAD_DIM)
        q = conv_silu(qp_ref, qc_ref, qn_ref, 0, hl)
        k = conv_silu(kp_ref, kc_ref, kn_ref, 1, hl)
        v = conv_silu(vp_ref, vc_ref, vn_ref, 2, hl)
        q = q * lax.rsqrt(jnp.sum(q * q, axis=-1, keepdims=True) + EPS) * qscale
        k = k * lax.rsqrt(jnp.sum(k * k, axis=-1, keepdims=True) + EPS)
        kb16 = k.astype(BF16)
        kk = lax.dot_general(kb16, kb16, nt_dims, preferred_element_type=F32)
        qk = lax.dot_general(q.astype(BF16), kb16, nt_dims, preferred_element_type=F32)
        for d in range(2):
            c = 2 * hd + d
            cb = d * N_HEADS + hd
            cg = 2 * N_HEADS + cb
            beta = beta_all[:, cb:cb + 1]
            gcc = gc[d][:, cg:cg + 1]
            gcr = gc_t[d][cg:cg + 1, :]
            decay = jnp.exp(jnp.where(incl[d], gcc - gcr, NEG_INF))
            lmat = jnp.where(offdiag, beta * kk * decay, 0.0)
            tb_scr[c] = (eye - jnp.where(level0, lmat, 0.0)).astype(BF16)
            lb_scr[c] = lmat.astype(BF16)
            egc = jnp.exp(gcc)
            rhs_scr[c] = jnp.concatenate([v * beta, k * (beta * egc)], axis=1).astype(BF16)
            _, _, qd_ref, kd_ref, a_ref = outs[d]
            qd_ref[:, hl] = (q * egc).astype(BF16)
            kd_ref[:, hl] = (k * jnp.exp(gtot[:, cg:cg + 1] - gcc)).astype(BF16)
            aqk = (qk * decay).astype(BF16)
            for cc in range(n // CHUNK):
                a_ref[cc * CHUNK:(cc + 1) * CHUNK, hd * CHUNK:(hd + 1) * CHUNK] = \
                    aqk[cc * CHUNK:(cc + 1) * CHUNK, cc * CHUNK:(cc + 1) * CHUNK]

    eye_b = eye.astype(BF16)

    def phase_b_rounds(chains):
        def first(lv):
            for c in chains:
                x = jnp.dot(lb_scr[c] * lvl_ref[lv], tb_scr[c], preferred_element_type=F32)
                x_scr[c] = x.astype(BF16) + eye_b

        def second():
            for c in chains:
                tb_scr[c] = jnp.dot(tb_scr[c], x_scr[c], preferred_element_type=F32).astype(BF16)

        rounds = []
        for lv in range(lvl_ref.shape[0]):
            rounds += [functools.partial(first, lv), second]
        return rounds

    def phase_c(chains):
        for c in chains:
            hd, d = divmod(c, 2)
            hl = slice(hd * HEAD_DIM, (hd + 1) * HEAD_DIM)
            sol = jnp.dot(tb_scr[c], rhs_scr[c], preferred_element_type=F32)
            outs[d][0][:, hl] = sol[:, :HEAD_DIM]
            outs[d][1][:, hl] = sol[:, HEAD_DIM:].astype(BF16)

    half = N_HEADS // 2
    group0 = list(range(0, 2 * half))
    group1 = list(range(2 * half, 2 * N_HEADS))
    for hd in range(half):
        phase_a(hd)
    rounds = phase_b_rounds(group0)
    per_head = -(-len(rounds) // half)
    for idx, hd in enumerate(range(half, N_HEADS)):
        for rnd in rounds[idx * per_head:(idx + 1) * per_head]:
            rnd()
        phase_a(hd)
    phase_c(group0)
    for rnd in phase_b_rounds(group1):
        rnd()
    phase_c(group1)


def _dn_prep(dq, dk, dv, gates, conv_w, alog_row, dtb_row):
    b, s, hw = dq.shape
    n = DN_TILE
    nb = s // n
    rh = n // HALO
    cur = pl.BlockSpec((None, n, hw), lambda bi, i: (bi, i, 0))
    prev = pl.BlockSpec((None, HALO, hw), lambda bi, i: (bi, jnp.maximum(i * rh - 1, 0), 0))
    nxt = pl.BlockSpec((None, HALO, hw), lambda bi, i: (bi, jnp.minimum((i + 1) * rh, s // HALO - 1), 0))
    small = pl.BlockSpec((None, n, LANES), lambda bi, i: (bi, i, 0))
    aspec = pl.BlockSpec((None, n, N_HEADS * CHUNK), lambda bi, i: (bi, i, 0))
    const = lambda a: pl.BlockSpec(a.shape, lambda bi, i: (0, 0))
    f32s = jax.ShapeDtypeStruct((b, s, hw), F32)
    b16s = jax.ShapeDtypeStruct((b, s, hw), BF16)
    a16s = jax.ShapeDtypeStruct((b, s, N_HEADS * CHUNK), BF16)
    per_dir_specs = [cur, cur, cur, cur, aspec]
    per_dir_shapes = [f32s, b16s, b16s, b16s, a16s]
    idx = np.arange(n)
    lvl = np.stack([((idx[:, None] // (2 * sz)) == (idx[None, :] // (2 * sz)))
                    & ((idx[:, None] // sz) != (idx[None, :] // sz)) for sz in (2, 4, 8, 16, 32)])
    lvl = jnp.asarray(-lvl.astype(np.float32)).astype(BF16)
    n_chain = 2 * N_HEADS
    return pl.pallas_call(
        _dn_prep_kernel,
        grid=(b, nb),
        in_specs=[prev, cur, nxt] * 3 + [small, const(conv_w), const(alog_row), const(dtb_row),
                                         pl.BlockSpec(lvl.shape, lambda bi, i: (0, 0, 0))],
        out_specs=per_dir_specs * 2 + [small],
        out_shape=per_dir_shapes * 2 + [jax.ShapeDtypeStruct((b, s, LANES), F32)],
        scratch_shapes=[pltpu.VMEM((n_chain, n, n), BF16),
                        pltpu.VMEM((n_chain, n, n), BF16), pltpu.VMEM((n_chain, n, n), BF16),
                        pltpu.VMEM((n_chain, n, 2 * HEAD_DIM), BF16)],
        compiler_params=_cparams(("parallel", "parallel")),
        name="deltanet_prep",
    )(dq, dq, dq, dk, dk, dk, dv, dv, dv, gates, conv_w, alog_row, dtb_row, lvl)


def _dn_scan_kernel(uf_ref, wf_ref, qf_ref, kf_ref, af_ref, glf_ref,
                    ub_ref, wb_ref, qb_ref, kb_ref, ab_ref, glb_ref,
                    of_ref, ob_ref, state_scr):
    @pl.when(pl.program_id(1) == 0)
    def _():
        state_scr[...] = jnp.zeros_like(state_scr)

    ins = ((uf_ref, wf_ref, qf_ref, kf_ref, af_ref, glf_ref, of_ref),
           (ub_ref, wb_ref, qb_ref, kb_ref, ab_ref, glb_ref, ob_ref))
    nchunks = SCAN_TILE // CHUNK

    chains = [(d, hd) for d in range(2) for hd in range(N_HEADS)]

    def body(ci, carry):
        rows, gl_rows = [], []
        for d in range(2):
            cidx = ci if d == 0 else nchunks - 1 - ci
            r0 = pl.multiple_of(cidx * CHUNK, CHUNK)
            rows.append(pl.ds(r0, CHUNK))
            gl_rows.append(ins[d][5][pl.ds(r0, 1), :])
        hls = [slice(hd * HEAD_DIM, (hd + 1) * HEAD_DIM) for hd in range(N_HEADS)]
        ws = []
        for d, hd in chains:
            _, w_ref, qd_ref = ins[d][:3]
            wq = jnp.concatenate([w_ref[rows[d], hls[hd]], qd_ref[rows[d], hls[hd]]], axis=0)
            ws.append(jnp.dot(wq, state_scr[d, hd].astype(BF16), preferred_element_type=F32))
        vbs = [(ins[d][0][rows[d], hls[hd]] - ws[i][:CHUNK]).astype(BF16) for i, (d, hd) in enumerate(chains)]
        for i, (d, hd) in enumerate(chains):
            kd_ref, a_ref, o_ref = ins[d][3], ins[d][4], ins[d][6]
            a = a_ref[rows[d], hd * CHUNK:(hd + 1) * CHUNK]
            o_ref[rows[d], hls[hd]] = (ws[i][CHUNK:] + jnp.dot(a, vbs[i], preferred_element_type=F32)
                                       ).astype(o_ref.dtype)
            upd = lax.dot_general(kd_ref[rows[d], hls[hd]], vbs[i], (((0,), (0,)), ((), ())),
                                  preferred_element_type=F32)
            cg = 2 * N_HEADS + d * N_HEADS + hd
            state_scr[d, hd] = state_scr[d, hd] * gl_rows[d][:, cg:cg + 1] + upd
        return carry

    lax.fori_loop(0, nchunks, body, 0)


def _dn_scan(prep):
    uf, wf, qf, kf, af, ub, wb, qb, kb, ab, gl = prep
    b, s, hw = uf.shape
    n = SCAN_TILE
    nb = s // n
    fwd = lambda w: pl.BlockSpec((None, n, w), lambda bi, i: (bi, i, 0))
    bwd = lambda w: pl.BlockSpec((None, n, w), lambda bi, i: (bi, nb - 1 - i, 0))
    widths = [hw, hw, hw, hw, N_HEADS * CHUNK, LANES]
    return pl.pallas_call(
        _dn_scan_kernel,
        grid=(b, nb),
        in_specs=[fwd(w) for w in widths] + [bwd(w) for w in widths],
        out_specs=[fwd(hw), bwd(hw)],
        out_shape=[jax.ShapeDtypeStruct((b, s, hw), BF16)] * 2,
        scratch_shapes=[pltpu.VMEM((2, N_HEADS, HEAD_DIM, HEAD_DIM), F32)],
        compiler_params=_cparams(("parallel", "arbitrary")),
        name="deltanet_scan",
    )(uf, wf, qf, kf, af, gl, ub, wb, qb, kb, ab, gl)


def _outproj_kernel(x_ref, mod_ref, oa_ref, of_ref, ob_ref, z_ref, na_ref, nd_ref, w_ref, o_ref):
    oa = jnp.concatenate([oa_ref[hd] for hd in range(N_HEADS)], axis=-1).astype(F32)
    oa = _rms(oa, na_ref[...])
    od = of_ref[...].astype(F32) + ob_ref[...].astype(F32)
    z = z_ref[...].astype(F32)
    parts = []
    for hd in range(N_HEADS):
        hl = slice(hd * HEAD_DIM, (hd + 1) * HEAD_DIM)
        parts.append(_rms(od[:, hl], nd_ref[...]) * _silu(z[:, hl]))
    y = jnp.concatenate([oa] + parts, axis=-1).astype(BF16)
    out = jnp.dot(y, w_ref[...], preferred_element_type=F32)
    o_ref[...] = x_ref[...] + mod_ref[5:6, :] * out


def _outproj(x, mod, o_attn, o_f, o_b, z, norm_attn, norm_dn, w_out):
    b, s, d = x.shape
    tm = ROW_TILE
    hw = HALF_WIDTH
    flat = pl.BlockSpec((None, tm, hw), lambda bi, i: (bi, i, 0))
    return pl.pallas_call(
        _outproj_kernel,
        grid=(b, s // tm),
        in_specs=[pl.BlockSpec((None, tm, d), lambda bi, i: (bi, i, 0)),
                  pl.BlockSpec((None, N_MOD, d), lambda bi, i: (bi, 0, 0)),
                  pl.BlockSpec((None, N_HEADS, tm, HEAD_DIM), lambda bi, i: (bi, 0, i, 0)),
                  flat, flat, flat,
                  pl.BlockSpec((1, hw), lambda bi, i: (0, 0)),
                  pl.BlockSpec((1, HEAD_DIM), lambda bi, i: (0, 0)),
                  pl.BlockSpec((d, d), lambda bi, i: (0, 0))],
        out_specs=pl.BlockSpec((None, tm, d), lambda bi, i: (bi, i, 0)),
        out_shape=jax.ShapeDtypeStruct((b, s, d), F32),
        compiler_params=_cparams(("parallel", "parallel")),
        name="outproj",
    )(x, mod, o_attn, o_f, o_b, z, norm_attn.reshape(1, hw), norm_dn.reshape(1, HEAD_DIM), w_out)


def _pad_lanes(vec, offset):
    flat = vec.reshape(-1).astype(F32)
    return jnp.zeros((1, LANES), F32).at[0, offset:offset + flat.shape[0]].set(flat)


def _trunk(x, mod, p):
    x = _ffn(x, mod, p["norm_ffn1"], p["w_ffn1_in"], p["w_ffn1_out"], p["norm_final"], sub=0, final=False)
    aq, ak, av, dq, dk, dv, z, gates = _inproj(x, mod, p["norm_mix"], p["w_in_main"], p["w_in_gate"])
    o_attn = _attention(aq, ak, av, p["bias_tables"])
    prep = _dn_prep(dq, dk, dv, gates, p["conv_w"], p["alog_row"], p["dtb_row"])
    o_f, o_b = _dn_scan(prep)
    x = _outproj(x, mod, o_attn, o_f, o_b, z, p["norm_attn_out"], p["norm_dn_out"], p["w_out"])
    return _ffn(x, mod, p["norm_ffn2"], p["w_ffn2_in"], p["w_ffn2_out"], p["norm_final"], sub=2, final=True)


def kernel(x_prompt, x_sample, c_prompt, c_sample, w_mod, b_mod, norm_ffn1, w_ffn1_in, w_ffn1_out, norm_mix, w_in, conv_w, a_log, dt_bias, norm_attn_out, norm_dn_out, w_out, norm_ffn2, w_ffn2_in, w_ffn2_out, rel_bias, norm_final):
    assert w_mod.shape[0] == 1, "one encoder layer"
    bp, bs = c_prompt.shape[0], c_sample.shape[0]
    n_gate = 4 * N_HEADS
    cut = 3 * HALF_WIDTH + 4 * HALF_WIDTH
    p = {
        "norm_ffn1": norm_ffn1[0], "w_ffn1_in": w_ffn1_in[0].astype(BF16), "w_ffn1_out": w_ffn1_out[0].astype(BF16),
        "norm_mix": norm_mix[0],
        "w_in_main": w_in[0, :, :cut].astype(BF16),
        "w_in_gate": jnp.pad(w_in[0, :, cut:], ((0, 0), (0, LANES - n_gate))).astype(BF16),
        "conv_w": conv_w[0],
        "alog_row": _pad_lanes(a_log[0], 2 * N_HEADS), "dtb_row": _pad_lanes(dt_bias[0], 2 * N_HEADS),
        "norm_attn_out": norm_attn_out[0], "norm_dn_out": norm_dn_out[0], "w_out": w_out[0].astype(BF16),
        "norm_ffn2": norm_ffn2[0], "w_ffn2_in": w_ffn2_in[0].astype(BF16), "w_ffn2_out": w_ffn2_out[0].astype(BF16),
        "norm_final": norm_final,
        "bias_tables": _attn_bias_tables(rel_bias),
    }
    c_all = jnp.concatenate([c_prompt, c_sample, jnp.zeros((8 - (bp + bs) % 8, D_MODEL), F32)], axis=0)
    mod = _modulation(c_all, w_mod[0], b_mod[0]).reshape(c_all.shape[0], N_MOD, D_MODEL)
    y_prompt = _trunk(x_prompt, mod[:bp], p)
    y_sample = _trunk(x_sample, mod[bp:bp + bs], p)
    return (y_prompt, y_sample)
```
